```python
import math
import jax
import jax.numpy as jnp
from jax import lax
import numpy as np

D_MODEL = 2048
BATCH = 8
SEQ = 4096
DEPTH = 4

CTX_LEN = 256
GRID_W = 64
N_MIXERS = 3
MIXER_OF_LAYER = tuple(i % N_MIXERS for i in range(DEPTH))
N_SSD_LAYERS = MIXER_OF_LAYER.count(0)
N_ATTN_LAYERS = MIXER_OF_LAYER.count(1)
N_HYENA_LAYERS = MIXER_OF_LAYER.count(2)

ALPHA = (2.0 * DEPTH) ** 0.25
BETA = (8.0 * DEPTH) ** -0.25
LN_EPS = 1e-5
RMS_EPS = 1e-5
N_MOD = 6
MLP_HIDDEN = 4 * D_MODEL

SSM_EXPAND = 2
D_INNER = SSM_EXPAND * D_MODEL
SSM_HEAD_DIM = 64
SSM_HEADS = D_INNER // SSM_HEAD_DIM
SSM_GROUPS = 8
SSM_HEADS_PER_GROUP = SSM_HEADS // SSM_GROUPS
SSM_STATE = 128
SSM_CONV_W = 5
SSM_CHUNK = 128
SSM_XBC = D_INNER + 2 * SSM_GROUPS * SSM_STATE
SSM_IN = D_INNER + SSM_XBC + 2 * SSM_HEADS

ATTN_HEAD_DIM = 64
ATTN_Q_HEADS = D_MODEL // ATTN_HEAD_DIM
ATTN_KV_HEADS = 4
ATTN_GROUP = ATTN_Q_HEADS // ATTN_KV_HEADS
ATTN_QKV = (ATTN_Q_HEADS + 2 * ATTN_KV_HEADS) * ATTN_HEAD_DIM
WINDOW = 128
ATTN_BLOCK = 128
ROPE_BASE = 10000.0

HYENA_ORDER = 2
HYENA_SHORT_W = 3
HYENA_EMB = 33
HYENA_FILTER_W = 64
HYENA_DECAY_FAST = 0.3
HYENA_DECAY_SLOW = 1.5
HYENA_DECAY_TARGET = 1e-2

kernel_name = 'hybrid_ssd_swa_hyena_dit_trunk'

F32 = jnp.float32


def layer_norm(x, g, b):
    xf = x.astype(F32)
    mu = jnp.mean(xf, axis=-1, keepdims=True)
    var = jnp.mean(jnp.square(xf - mu), axis=-1, keepdims=True)
    y = (xf - mu) * lax.rsqrt(var + LN_EPS) * g.astype(F32) + b.astype(F32)
    return y.astype(x.dtype)


def dwconv_centred(x, w, b):
    width = w.shape[0]
    y = lax.conv_general_dilated(
        x, w[:, None, :].astype(x.dtype), window_strides=(1,),
        padding=[(width // 2, width // 2)],
        dimension_numbers=('NWC', 'WIO', 'NWC'), feature_group_count=x.shape[-1])
    return y + b


def sq_relu_mlp(u, w1, w2):
    return jnp.square(jax.nn.relu(u @ w1)) @ w2


def ssd_chunk_scan(xs, dt, a, bm, cm, h0):
    b, L = xs.shape[0], xs.shape[1]
    nc = L // SSM_CHUNK

    def to_chunks(t):
        return jnp.swapaxes(t.reshape((b, nc, SSM_CHUNK) + t.shape[2:]), 0, 1)

    lower = jnp.tril(jnp.ones((SSM_CHUNK, SSM_CHUNK), dtype=bool))[None, :, :, None, None]

    def step(h, inp):
        x_c, dt_c, b_c, c_c = inp
        x_c = x_c.astype(F32)
        b_c = b_c.astype(F32)
        c_c = c_c.astype(F32)
        cum = jnp.cumsum(dt_c * a, axis=1)
        decay = jnp.exp(jnp.where(lower, cum[:, :, None] - cum[:, None, :], -jnp.inf))
        cb = jnp.einsum('bign,bjgn->bijg', c_c, b_c)
        y = jnp.einsum('bijgh,bjghp->bighp', cb[..., None] * decay, x_c * dt_c[..., None])
        y = y + jnp.einsum('bign,bghpn->bighp', c_c, h) * jnp.exp(cum)[..., None]
        w = jnp.exp(cum[:, -1:] - cum) * dt_c
        h = h * jnp.exp(cum[:, -1])[..., None, None] + jnp.einsum('bjgh,bjghp,bjgn->bghpn', w, x_c, b_c)
        return h, y

    h, ys = lax.scan(step, h0, (to_chunks(xs), to_chunks(dt), to_chunks(bm), to_chunks(cm)))
    return jnp.swapaxes(ys, 0, 1).reshape(xs.shape), h


def ssd_mixer(u_lat, u_ctx, w_in, conv_w, conv_b, dt_bias, a_log, d_skip, norm_g, w_out, ctx_out):
    G, Hg, P, N = SSM_GROUPS, SSM_HEADS_PER_GROUP, SSM_HEAD_DIM, SSM_STATE
    a = -jnp.exp(a_log.astype(F32)).reshape(2, G, Hg)

    def project(u):
        b, n, _ = u.shape
        z, xbc, dt = jnp.split(u @ w_in, [D_INNER, D_INNER + SSM_XBC], axis=-1)
        xbc = jax.nn.silu(dwconv_centred(xbc, conv_w, conv_b))
        xs, bm, cm = jnp.split(xbc, [D_INNER, D_INNER + G * N], axis=-1)
        dt = jax.nn.softplus(dt.astype(F32).reshape(b, n, 2, SSM_HEADS) + dt_bias.astype(F32))
        return (z, xs.reshape(b, n, G, Hg, P), bm.reshape(b, n, G, N), cm.reshape(b, n, G, N),
                dt.reshape(b, n, 2, G, Hg))

    def rev(t):
        return jnp.flip(t, axis=1)

    def bidir(xs, bm, cm, dt, h_f, h_b):
        y_f, s_f = ssd_chunk_scan(xs, dt[:, :, 0], a[0], bm, cm, h_f)
        y_b, s_b = ssd_chunk_scan(rev(xs), rev(dt[:, :, 1]), a[1], rev(bm), rev(cm), h_b)
        return y_f + rev(y_b), s_f, s_b

    def finish(y, xs, z):
        b, n = y.shape[0], y.shape[1]
        y = y + d_skip.astype(F32).reshape(G, Hg)[..., None] * xs.astype(F32)
        y = y.reshape(b, n, D_INNER) * jax.nn.silu(z.astype(F32))
        yg = y.reshape(b, n, G, D_INNER // G)
        yg = yg * lax.rsqrt(jnp.mean(jnp.square(yg), axis=-1, keepdims=True) + RMS_EPS)
        y = (yg.reshape(b, n, D_INNER) * norm_g.astype(F32)).astype(u_lat.dtype)
        return y @ w_out

    zc, xc, bc, cc, dtc = project(u_ctx)
    zl, xl, bl, cl, dtl = project(u_lat)
    h0 = jnp.zeros((u_lat.shape[0], G, Hg, P, N), F32)
    yc, hc_f, hc_b = bidir(xc, bc, cc, dtc, h0, h0)
    yl, _, _ = bidir(xl, bl, cl, dtl, hc_f, hc_b)
    y_lat = finish(yl, xl, zl)
    y_ctx = finish(yc, xc, zc) if ctx_out else None
    return y_lat, y_ctx


def axial_rope_angles(L):
    rows = L // GRID_W
    row_id = jnp.broadcast_to(jnp.arange(rows, dtype=F32)[:, None], (rows, GRID_W)).reshape(-1)
    col_id = jnp.broadcast_to(jnp.arange(GRID_W, dtype=F32)[None, :], (rows, GRID_W)).reshape(-1)
    pairs_per_axis = ATTN_HEAD_DIM // 4
    inv = ROPE_BASE ** (-jnp.arange(pairs_per_axis, dtype=F32) / pairs_per_axis)
    ang = jnp.concatenate([row_id[:, None] * inv, col_id[:, None] * inv], axis=-1)
    return jnp.cos(ang), jnp.sin(ang)


def apply_rope(x, cos, sin):
    xf = x.astype(F32).reshape(x.shape[:-1] + (x.shape[-1] // 2, 2))
    x1, x2 = xf[..., 0], xf[..., 1]
    out = jnp.stack([x1 * cos - x2 * sin, x1 * sin + x2 * cos], axis=-1)
    return out.reshape(x.shape).astype(x.dtype)


def windowed_gqa_mixer(u_lat, u_ctx, w_qkv, sink, w_o, ctx_out):
    b, L, _ = u_lat.shape
    KV, G, Dh, BLK = ATTN_KV_HEADS, ATTN_GROUP, ATTN_HEAD_DIM, ATTN_BLOCK
    scale = Dh ** -0.5

    def project(u):
        n = u.shape[1]
        q, k, v = jnp.split(u @ w_qkv, [ATTN_Q_HEADS * Dh, (ATTN_Q_HEADS + KV) * Dh], axis=-1)
        return (q.reshape(b, n, KV, G, Dh) * scale, k.reshape(b, n, KV, Dh), v.reshape(b, n, KV, Dh))

    q_l, k_l, v_l = project(u_lat)
    cos, sin = axial_rope_angles(L)
    q_l = apply_rope(q_l, cos[:, None, None], sin[:, None, None])
    k_l = apply_rope(k_l, cos[:, None], sin[:, None])
    q_c, k_c, v_c = project(u_ctx)
    sink_b = sink.astype(F32).reshape(KV, G)[:, :, None, None]

    nblk = L // BLK
    qb = jnp.swapaxes(q_l.reshape(b, nblk, BLK, KV, G, Dh), 0, 1)

    def windows(t):
        tp = jnp.pad(t, ((0, 0), (BLK, BLK), (0, 0), (0, 0))).reshape(b, nblk + 2, BLK, KV, Dh)
        w = jnp.concatenate([tp[:, :-2], tp[:, 1:-1], tp[:, 2:]], axis=2)
        return jnp.swapaxes(w, 0, 1)

    kw, vw = windows(k_l), windows(v_l)
    r_idx = jnp.arange(BLK)[:, None]
    s_idx = jnp.arange(3 * BLK)[None, :]
    band = jnp.abs(s_idx - BLK - r_idx) <= WINDOW

    def block_attend(args):
        q, k, v, n = args
        kpos = (n - 1) * BLK + s_idx
        mask = band & (kpos >= 0) & (kpos < L)
        s_loc = jnp.einsum('bqhgd,bshd->bhgqs', q, k).astype(F32)
        s_loc = jnp.where(mask, s_loc, -jnp.inf)
        s_ctx = jnp.einsum('bqhgd,bchd->bhgqc', q, k_c).astype(F32)
        logits = jnp.concatenate(
            [s_loc, s_ctx, jnp.broadcast_to(sink_b, s_loc.shape[:-1] + (1,))], axis=-1)
        p = jax.nn.softmax(logits, axis=-1).astype(v.dtype)
        o = jnp.einsum('bhgqs,bshd->bqhgd', p[..., :3 * BLK], v)
        o = o + jnp.einsum('bhgqc,bchd->bqhgd', p[..., 3 * BLK:-1], v_c)
        return o

    o = lax.map(block_attend, (qb, kw, vw, jnp.arange(nblk)))
    y_lat = jnp.swapaxes(o, 0, 1).reshape(b, L, ATTN_Q_HEADS * Dh) @ w_o
    y_ctx = None
    if ctx_out:
        s_c = jnp.einsum('bqhgd,bchd->bhgqc', q_c, k_c).astype(F32)
        logits = jnp.concatenate([s_c, jnp.broadcast_to(sink_b, s_c.shape[:-1] + (1,))], axis=-1)
        p = jax.nn.softmax(logits, axis=-1)[..., :-1].astype(v_c.dtype)
        o_c = jnp.einsum('bhgqc,bchd->bqhgd', p, v_c)
        y_ctx = o_c.reshape(b, u_ctx.shape[1], ATTN_Q_HEADS * Dh) @ w_o
    return y_lat, y_ctx


def hyena_filter_spectra(L, w1, b1, w2, b2, w3, b3, w4, freq):
    t = jnp.linspace(0.0, 1.0, L, dtype=F32)[:, None]
    bands = (HYENA_EMB - 1) // 2
    w = 2.0 * math.pi * jnp.arange(L, dtype=F32)[:, None] / L
    f = jnp.linspace(1e-4, bands - 1, bands, dtype=F32)[None, :]
    z = jnp.concatenate([t, jnp.cos(f * w), -jnp.sin(f * w)], axis=-1)
    fr = freq.astype(F32)
    h = jnp.sin(fr * (z @ w1.astype(F32) + b1.astype(F32)))
    h = jnp.sin(fr * (h @ w2.astype(F32) + b2.astype(F32)))
    h = jnp.sin(fr * (h @ w3.astype(F32) + b3.astype(F32)))
    h = (h @ w4.astype(F32)).reshape(L, HYENA_ORDER, 2, D_MODEL)
    max_decay = math.log(HYENA_DECAY_TARGET) / HYENA_DECAY_FAST
    min_decay = math.log(HYENA_DECAY_TARGET) / HYENA_DECAY_SLOW
    deltas = jnp.linspace(min_decay, max_decay, D_MODEL, dtype=F32)
    h = h * jnp.exp(-t * jnp.abs(deltas))[:, None, None, :]
    fwd = h[:, :, 0]
    bwd = jnp.flip(h[1:, :, 1], axis=0)
    full = jnp.concatenate([fwd, jnp.zeros((1, HYENA_ORDER, D_MODEL), F32), bwd], axis=0)
    return jnp.fft.rfft(full, axis=0)


def fft_long_conv(u, spec, bias):
    L = u.shape[1]
    uf = u.astype(F32)
    y = jnp.fft.irfft(jnp.fft.rfft(uf, n=2 * L, axis=1) * spec, n=2 * L, axis=1)[:, :L]
    return (y + uf * bias.astype(F32)).astype(u.dtype)


def hyena_mixer(u_lat, u_ctx, w_in, conv_w, conv_b, f_w1, f_b1, f_w2, f_b2, f_w3, f_b3, f_w4,
                f_freq, f_bias, w_out, ctx_out):
    def run(u):
        spec = hyena_filter_spectra(u.shape[1], f_w1, f_b1, f_w2, f_b2, f_w3, f_b3, f_w4, f_freq)
        x1, x2, v = jnp.split(dwconv_centred(u @ w_in, conv_w, conv_b), 3, axis=-1)
        z = x1 * fft_long_conv(v, spec[:, 0], f_bias[0])
        y = x2 * fft_long_conv(z, spec[:, 1], f_bias[1])
        return y @ w_out

    return run(u_lat), (run(u_ctx) if ctx_out else None)


def setup_inputs(seed: int = 0) -> dict:
    key = jax.random.key(seed)
    ks = iter(list(jax.random.split(key, 48)))

    def nrm(shape, scale=1.0):
        return jax.random.normal(next(ks), shape, F32) * scale

    D = D_MODEL
    NA, NB, NC = N_SSD_LAYERS, N_ATTN_LAYERS, N_HYENA_LAYERS
    dt0 = jnp.exp(jax.random.uniform(next(ks), (NA, 2, SSM_HEADS), F32,
                                     minval=math.log(1e-3), maxval=math.log(1e-1)))
    dt_bias = dt0 + jnp.log(-jnp.expm1(-dt0))
    a_log = jnp.log(jax.random.uniform(next(ks), (NA, 2, SSM_HEADS), F32, minval=1.0, maxval=16.0))
    return {
        'x': nrm((BATCH, SEQ, D)),
        'c': nrm((BATCH, D)),
        'ctx': nrm((BATCH, CTX_LEN, D)),
        'c_ctx': nrm((D,)),
        'ada_w': nrm((DEPTH, D, N_MOD * D), D ** -0.5),
        'ada_b': nrm((DEPTH, N_MOD * D), 0.02),
        'ln_g': 1.0 + nrm((DEPTH, 2, D), 0.02),
        'ln_b': nrm((DEPTH, 2, D), 0.02),
        'mlp_w1': nrm((DEPTH, D, MLP_HIDDEN), D ** -0.5),
        'mlp_w2': nrm((DEPTH, MLP_HIDDEN, D), BETA * MLP_HIDDEN ** -0.5),
        'ssd_w_in': nrm((NA, D, SSM_IN), D ** -0.5),
        'ssd_conv_w': nrm((NA, SSM_CONV_W, SSM_XBC), SSM_CONV_W ** -0.5),
        'ssd_conv_b': nrm((NA, SSM_XBC), 0.01),
        'ssd_dt_bias': dt_bias,
        'ssd_a_log': a_log,
        'ssd_d': 1.0 + nrm((NA, SSM_HEADS), 0.02),
        'ssd_norm_g': 1.0 + nrm((NA, D_INNER), 0.02),
        'ssd_w_out': nrm((NA, D_INNER, D), BETA * D_INNER ** -0.5),
        'attn_w_qkv': nrm((NB, D, ATTN_QKV), D ** -0.5),
        'attn_sink': nrm((NB, ATTN_Q_HEADS), 0.5),
        'attn_w_o': nrm((NB, ATTN_Q_HEADS * ATTN_HEAD_DIM, D), BETA * (ATTN_Q_HEADS * ATTN_HEAD_DIM) ** -0.5),
        'hy_w_in': nrm((NC, D, 3 * D), D ** -0.5),
        'hy_conv_w': nrm((NC, HYENA_SHORT_W, 3 * D), HYENA_SHORT_W ** -0.5),
        'hy_conv_b': nrm((NC, 3 * D), 0.01),
        'hy_f_w1': nrm((NC, HYENA_EMB, HYENA_FILTER_W), HYENA_EMB ** -0.5),
        'hy_f_b1': nrm((NC, HYENA_FILTER_W), 0.02),
        'hy_f_w2': nrm((NC, HYENA_FILTER_W, HYENA_FILTER_W), HYENA_FILTER_W ** -0.5),
        'hy_f_b2': nrm((NC, HYENA_FILTER_W), 0.02),
        'hy_f_w3': nrm((NC, HYENA_FILTER_W, HYENA_FILTER_W), HYENA_FILTER_W ** -0.5),
        'hy_f_b3': nrm((NC, HYENA_FILTER_W), 0.02),
        'hy_f_w4': nrm((NC, HYENA_FILTER_W, HYENA_ORDER * 2 * D), 0.1 * HYENA_FILTER_W ** -0.5),
        'hy_f_freq': 1.0 + nrm((NC, HYENA_FILTER_W), 0.01),
        'hy_f_bias': nrm((NC, HYENA_ORDER, D), 0.5),
        'hy_w_out': nrm((NC, D, D), BETA * D ** -0.5),
    }


def reference(x, c, ctx, c_ctx, ada_w, ada_b, ln_g, ln_b, mlp_w1, mlp_w2,
              ssd_w_in, ssd_conv_w, ssd_conv_b, ssd_dt_bias, ssd_a_log, ssd_d, ssd_norm_g, ssd_w_out,
              attn_w_qkv, attn_sink, attn_w_o,
              hy_w_in, hy_conv_w, hy_conv_b, hy_f_w1, hy_f_b1, hy_f_w2, hy_f_b2, hy_f_w3, hy_f_b3,
              hy_f_w4, hy_f_freq, hy_f_bias, hy_w_out):
    xl, xc = x, ctx
    cond_l = jax.nn.silu(c.astype(F32)).astype(x.dtype)
    cond_c = jax.nn.silu(c_ctx.astype(F32)).astype(x.dtype)
    for i in range(DEPTH):
        last = i == DEPTH - 1
        kind = MIXER_OF_LAYER[i]
        j = MIXER_OF_LAYER[:i].count(kind)
        mod_l = (cond_l @ ada_w[i] + ada_b[i])[:, None, :]
        mod_c = cond_c @ ada_w[i] + ada_b[i]
        sh1, sc1, g1, sh2, sc2, g2 = jnp.split(mod_l, N_MOD, axis=-1)
        csh1, csc1, cg1, csh2, csc2, cg2 = jnp.split(mod_c, N_MOD, axis=-1)
        ul = xl * (1.0 + sc1) + sh1
        uc = xc * (1.0 + csc1) + csh1
        if kind == 0:
            yl, yc = ssd_mixer(ul, uc, ssd_w_in[j], ssd_conv_w[j], ssd_conv_b[j], ssd_dt_bias[j],
                               ssd_a_log[j], ssd_d[j], ssd_norm_g[j], ssd_w_out[j], not last)
        elif kind == 1:
            yl, yc = windowed_gqa_mixer(ul, uc, attn_w_qkv[j], attn_sink[j], attn_w_o[j], not last)
        else:
            yl, yc = hyena_mixer(ul, uc, hy_w_in[j], hy_conv_w[j], hy_conv_b[j], hy_f_w1[j], hy_f_b1[j],
                                 hy_f_w2[j], hy_f_b2[j], hy_f_w3[j], hy_f_b3[j], hy_f_w4[j],
                                 hy_f_freq[j], hy_f_bias[j], hy_w_out[j], not last)
        xl = layer_norm(ALPHA * xl + g1 * yl, ln_g[i, 0], ln_b[i, 0])
        ml = sq_relu_mlp(xl * (1.0 + sc2) + sh2, mlp_w1[i], mlp_w2[i])
        xl = layer_norm(ALPHA * xl + g2 * ml, ln_g[i, 1], ln_b[i, 1])
        if not last:
            xc = layer_norm(ALPHA * xc + cg1 * yc, ln_g[i, 0], ln_b[i, 0])
            mc = sq_relu_mlp(xc * (1.0 + csc2) + csh2, mlp_w1[i], mlp_w2[i])
            xc = layer_norm(ALPHA * xc + cg2 * mc, ln_g[i, 1], ln_b[i, 1])
    return xl
```

```python
import functools
import math

import numpy as np
import jax
import jax.numpy as jnp
from jax import lax
from jax.experimental import pallas as pl
from jax.experimental.pallas import tpu as pltpu

F32 = jnp.float32
BF16 = jnp.bfloat16
HIGHEST = lax.Precision.HIGHEST

D_MODEL = 2048
DEPTH = 4
GRID_W = 64
N_MIXERS = 3
MIXER_OF_LAYER = tuple(i % N_MIXERS for i in range(DEPTH))
ALPHA = (2.0 * DEPTH) ** 0.25
LN_EPS = 1e-5
RMS_EPS = 1e-5
N_MOD = 6
MLP_HIDDEN = 4 * D_MODEL

D_INNER = 2 * D_MODEL
SSM_HEAD_DIM = 64
SSM_HEADS = D_INNER // SSM_HEAD_DIM
SSM_GROUPS = 8
SSM_HPG = SSM_HEADS // SSM_GROUPS
SSM_STATE = 128
SSM_CONV_W = 5
SSM_CHUNK = 128
SSM_XBC = D_INNER + 2 * SSM_GROUPS * SSM_STATE
SSM_GW = D_INNER // SSM_GROUPS

ATTN_HEAD_DIM = 64
ATTN_Q_HEADS = D_MODEL // ATTN_HEAD_DIM
ATTN_KV_HEADS = 4
ATTN_GROUP = ATTN_Q_HEADS // ATTN_KV_HEADS
ATTN_Q_W = ATTN_Q_HEADS * ATTN_HEAD_DIM
ATTN_KV_W = ATTN_KV_HEADS * ATTN_HEAD_DIM
WINDOW = 128
ATTN_BLOCK = 128
ROPE_BASE = 10000.0

HYENA_ORDER = 2
HYENA_SHORT_W = 3
HYENA_EMB = 33
HYENA_FILTER_W = 64
HYENA_DECAY_FAST = 0.3
HYENA_DECAY_SLOW = 1.5
HYENA_DECAY_TARGET = 1e-2

LANE = 128
FFT_SLAB = 128
COND_ROWS = 16
VMEM_LIMIT_MB = 56


def _cparams(sem, vmem_mb=VMEM_LIMIT_MB):
    return pltpu.CompilerParams(dimension_semantics=sem,
                                vmem_limit_bytes=vmem_mb * 1024 * 1024)


def _dot(a, b, precision=None):
    return jnp.dot(a, b, preferred_element_type=F32, precision=precision)


def _dot_nt(a, b):
    return lax.dot_general(a, b, (((1,), (1,)), ((), ())), preferred_element_type=F32)


def _dot_tn(a, b):
    return lax.dot_general(a, b, (((0,), (0,)), ((), ())), preferred_element_type=F32)


def _silu(x):
    return x * jax.nn.sigmoid(x)


def _ada_kernel(c_ref, w_ref, b_ref, o_ref):
    s = _silu(c_ref[...])
    o_ref[0] = _dot(s, w_ref[0], HIGHEST) + b_ref[0]


def ada_mods(cond, ada_w, ada_b):
    depth, d, n6 = ada_w.shape
    tn = 512
    return pl.pallas_call(
        _ada_kernel,
        grid=(depth, n6 // tn),
        in_specs=[pl.BlockSpec((COND_ROWS, d), lambda l, j: (0, 0)),
                  pl.BlockSpec((1, d, tn), lambda l, j: (l, 0, j)),
                  pl.BlockSpec((1, 1, tn), lambda l, j: (l, 0, j))],
        out_specs=pl.BlockSpec((1, COND_ROWS, tn), lambda l, j: (l, 0, j)),
        out_shape=jax.ShapeDtypeStruct((depth, COND_ROWS, n6), F32),
        compiler_params=_cparams(("arbitrary", "arbitrary")),
        name="ada_mods",
    )(cond, ada_w, ada_b.reshape(depth, 1, n6))


def _modulate_kernel(x_ref, sc_ref, sh_ref, o_ref):
    o_ref[0] = (x_ref[0] * (1.0 + sc_ref[0]) + sh_ref[0]).astype(o_ref.dtype)


def modulate(x, sc, sh):
    b, n, d = x.shape
    tm = min(n, 1024)
    row = pl.BlockSpec((1, tm, d), lambda i, j: (i, j, 0))
    vec = pl.BlockSpec((1, 1, d), lambda i, j: (i, 0, 0))
    return pl.pallas_call(
        _modulate_kernel, grid=(b, n // tm),
        in_specs=[row, vec, vec], out_specs=row,
        out_shape=jax.ShapeDtypeStruct((b, n, d), BF16),
        compiler_params=_cparams(("arbitrary", "arbitrary")),
        name="modulate",
    )(x, sc, sh)


def _mm_kernel(a_ref, w_ref, o_ref):
    o_ref[0] = _dot(a_ref[0], w_ref[...]).astype(o_ref.dtype)


def matmul(a, w, out_dtype, tn):
    b, n, k = a.shape
    nn = w.shape[1]
    tm = min(n, 1024)
    tn = min(tn, nn)
    return pl.pallas_call(
        _mm_kernel, grid=(b, n // tm, nn // tn),
        in_specs=[pl.BlockSpec((1, tm, k), lambda i, j, l: (i, j, 0)),
                  pl.BlockSpec((k, tn), lambda i, j, l: (0, l))],
        out_specs=pl.BlockSpec((1, tm, tn), lambda i, j, l: (i, j, l)),
        out_shape=jax.ShapeDtypeStruct((b, n, nn), out_dtype),
        compiler_params=_cparams(("arbitrary", "arbitrary", "arbitrary")),
        name="matmul",
    )(a, w)


def _res_ln(x, y, gate, g, b):
    h = ALPHA * x + gate * y
    mu = jnp.mean(h, axis=-1, keepdims=True)
    hc = h - mu
    var = jnp.mean(hc * hc, axis=-1, keepdims=True)
    return hc * lax.rsqrt(var + LN_EPS) * g + b


def _mm_res_ln_kernel(a_ref, w_ref, x_ref, gate_ref, g_ref, b_ref, sc_ref, sh_ref,
                      xo_ref, uo_ref, acc_ref):
    k = pl.program_id(2)

    @pl.when(k == 0)
    def _():
        acc_ref[...] = jnp.zeros_like(acc_ref)

    acc_ref[...] += _dot(a_ref[0], w_ref[...])

    @pl.when(k == pl.num_programs(2) - 1)
    def _():
        xn = _res_ln(x_ref[0], acc_ref[...], gate_ref[0], g_ref[...], b_ref[...])
        xo_ref[0] = xn
        uo_ref[0] = (xn * (1.0 + sc_ref[0]) + sh_ref[0]).astype(uo_ref.dtype)


def matmul_res_ln(a, w, x, gate, ln_g, ln_b, sc_next, sh_next):
    b, n, kk = a.shape
    d = w.shape[1]
    tm = min(n, 512)
    tk = min(kk, 1024)
    row = pl.BlockSpec((1, tm, d), lambda i, j, k: (i, j, 0))
    vec = pl.BlockSpec((1, 1, d), lambda i, j, k: (i, 0, 0))
    par = pl.BlockSpec((1, d), lambda i, j, k: (0, 0))
    return pl.pallas_call(
        _mm_res_ln_kernel, grid=(b, n // tm, kk // tk),
        in_specs=[pl.BlockSpec((1, tm, tk), lambda i, j, k: (i, j, k)),
                  pl.BlockSpec((tk, d), lambda i, j, k: (k, 0)),
                  row, vec, par, par, vec, vec],
        out_specs=[row, row],
        out_shape=[jax.ShapeDtypeStruct((b, n, d), F32), jax.ShapeDtypeStruct((b, n, d), BF16)],
        scratch_shapes=[pltpu.VMEM((tm, d), F32)],
        compiler_params=_cparams(("arbitrary", "arbitrary", "arbitrary")),
        name="matmul_res_ln",
    )(a, w, x, gate, ln_g.reshape(1, d), ln_b.reshape(1, d), sc_next, sh_next)


def _mlp_kernel(u_ref, w1_ref, w2_ref, x_ref, gate_ref, g_ref, b_ref, sc_ref, sh_ref,
                xo_ref, uo_ref, acc_ref):
    k = pl.program_id(2)

    @pl.when(k == 0)
    def _():
        acc_ref[...] = jnp.zeros_like(acc_ref)

    h = jnp.maximum(_dot(u_ref[0], w1_ref[...]), 0.0)
    acc_ref[...] += _dot((h * h).astype(BF16), w2_ref[...])

    @pl.when(k == pl.num_programs(2) - 1)
    def _():
        xn = _res_ln(x_ref[0], acc_ref[...], gate_ref[0], g_ref[...], b_ref[...])
        xo_ref[0] = xn
        uo_ref[0] = (xn * (1.0 + sc_ref[0]) + sh_ref[0]).astype(uo_ref.dtype)


def mlp_res_ln(u, w1, w2, x, gate, ln_g, ln_b, sc_next, sh_next):
    b, n, d = u.shape
    hid = w1.shape[1]
    tm = min(n, 512)
    th = 512
    row = pl.BlockSpec((1, tm, d), lambda i, j, k: (i, j, 0))
    vec = pl.BlockSpec((1, 1, d), lambda i, j, k: (i, 0, 0))
    par = pl.BlockSpec((1, d), lambda i, j, k: (0, 0))
    return pl.pallas_call(
        _mlp_kernel, grid=(b, n // tm, hid // th),
        in_specs=[row,
                  pl.BlockSpec((d, th), lambda i, j, k: (0, k)),
                  pl.BlockSpec((th, d), lambda i, j, k: (k, 0)),
                  row, vec, par, par, vec, vec],
        out_specs=[row, row],
        out_shape=[jax.ShapeDtypeStruct((b, n, d), F32), jax.ShapeDtypeStruct((b, n, d), BF16)],
        scratch_shapes=[pltpu.VMEM((tm, d), F32)],
        compiler_params=_cparams(("arbitrary", "arbitrary", "arbitrary")),
        name="mlp_res_ln",
    )(u, w1, w2, x, gate, ln_g.reshape(1, d), ln_b.reshape(1, d), sc_next, sh_next)


def _dwconv_kernel(x_ref, w_ref, b_ref, o_ref, *, width, act):
    x = x_ref[0].astype(F32)
    n = x.shape[0]
    half = width // 2
    rows = lax.broadcasted_iota(jnp.int32, x.shape, 0)
    acc = x * w_ref[half:half + 1, :] + b_ref[...]
    for k in range(width):
        off = k - half
        if off == 0:
            continue
        shifted = pltpu.roll(x, (-off) % n, axis=0)
        valid = (rows + off >= 0) & (rows + off < n)
        acc = acc + jnp.where(valid, shifted, 0.0) * w_ref[k:k + 1, :]
    if act:
        acc = _silu(acc)
    o_ref[0] = acc.astype(o_ref.dtype)


def dwconv(x, w, bias, col0, ncols, act, out_dtype=BF16):
    b, n, _ = x.shape
    width = w.shape[0]
    tc = 256 if n > 1024 else 512
    tc = min(tc, ncols)
    cb0 = col0 // tc
    return pl.pallas_call(
        functools.partial(_dwconv_kernel, width=width, act=act),
        grid=(b, ncols // tc),
        in_specs=[pl.BlockSpec((1, n, tc), lambda i, j: (i, 0, cb0 + j)),
                  pl.BlockSpec((width, tc), lambda i, j: (0, j)),
                  pl.BlockSpec((1, tc), lambda i, j: (0, j))],
        out_specs=pl.BlockSpec((1, n, tc), lambda i, j: (i, 0, j)),
        out_shape=jax.ShapeDtypeStruct((b, n, ncols), out_dtype),
        compiler_params=_cparams(("arbitrary", "arbitrary")),
        name="dwconv",
    )(x, w, bias.reshape(1, ncols))


def _dtprep_kernel(raw_ref, bias_ref, a_ref, dt_ref, cum_ref, cumt_ref):
    x = raw_ref[0] + bias_ref[...]
    dt = jnp.maximum(x, 0.0) + jnp.log1p(jnp.exp(-jnp.abs(x)))
    adt = dt * a_ref[...]
    q = x.shape[0]
    r = lax.broadcasted_iota(jnp.int32, (q, q), 0)
    c = lax.broadcasted_iota(jnp.int32, (q, q), 1)
    fwd = _dot((c <= r).astype(F32), adt, HIGHEST)
    bwd = _dot((c >= r).astype(F32), adt, HIGHEST)
    lane = lax.broadcasted_iota(jnp.int32, x.shape, 1)
    cum = jnp.where(lane < SSM_HEADS, fwd, bwd)
    dt_ref[0] = dt
    cum_ref[0] = cum
    cumt_ref[0] = cum.T


def ssd_dtprep(dt_raw, dt_bias, a_log):
    b, n, w = dt_raw.shape
    q = SSM_CHUNK
    a = -jnp.exp(a_log.astype(F32)).reshape(1, w)
    blk = pl.BlockSpec((1, q, w), lambda i, j: (i, j, 0))
    par = pl.BlockSpec((1, w), lambda i, j: (0, 0))
    return pl.pallas_call(
        _dtprep_kernel, grid=(b, n // q),
        in_specs=[blk, par, par],
        out_specs=[blk, blk, pl.BlockSpec((1, w, q), lambda i, j: (i, j, 0))],
        out_shape=[jax.ShapeDtypeStruct((b, n, w), F32), jax.ShapeDtypeStruct((b, n, w), F32),
                   jax.ShapeDtypeStruct((b, (n // q) * w, q), F32)],
        compiler_params=_cparams(("arbitrary", "arbitrary")),
        name="ssd_dtprep",
    )(dt_raw, dt_bias.reshape(1, w).astype(F32), a)


def _ssd_scan_kernel(*refs, reverse, finish):
    if finish:
        (x_ref, b_ref, c_ref, dt_ref, cum_ref, cumt_ref, h0_ref,
         yf_ref, z_ref, dsk_ref, ng_ref, y_ref, hout_ref, h_scr) = refs
    else:
        (x_ref, b_ref, c_ref, dt_ref, cum_ref, cumt_ref, h0_ref,
         y_ref, hout_ref, h_scr) = refs

    @pl.when(pl.program_id(2) == 0)
    def _():
        h_scr[...] = h0_ref[0, 0]

    q = SSM_CHUNK
    p = SSM_HEAD_DIM
    x = x_ref[0].astype(F32)
    bm = b_ref[0]
    cm = c_ref[0]
    dt = dt_ref[0, 0]
    cum = cum_ref[0, 0]
    cumt = cumt_ref[0, 0, 0]
    cb = _dot_nt(cm, bm)
    ri = lax.broadcasted_iota(jnp.int32, (q, q), 0)
    ci = lax.broadcasted_iota(jnp.int32, (q, q), 1)
    keep = (ci >= ri) if reverse else (ci <= ri)
    tot = cum[0:1, :] if reverse else cum[q - 1:q, :]
    h = h_scr[...]
    yoff = _dot_nt(cm, h.astype(BF16))
    ecum = jnp.exp(cum)
    wj = jnp.exp(tot - cum) * dt
    left = lax.broadcasted_iota(jnp.int32, (q, 2 * p), 1) < p

    ys, xws = [], []
    for pr in range(SSM_HPG // 2):
        ha, hb = 2 * pr, 2 * pr + 1
        xp = x[:, pr * 2 * p:(pr + 1) * 2 * p]
        xdt = xp * jnp.where(left, dt[:, ha:ha + 1], dt[:, hb:hb + 1])
        yp = yoff[:, pr * 2 * p:(pr + 1) * 2 * p] * jnp.where(left, ecum[:, ha:ha + 1], ecum[:, hb:hb + 1])
        for hd, sel in ((ha, left), (hb, jnp.logical_not(left))):
            dec = jnp.exp(jnp.where(keep, cum[:, hd:hd + 1] - cumt[hd:hd + 1, :], -jnp.inf))
            m = (cb * dec).astype(BF16)
            yp = yp + _dot(m, jnp.where(sel, xdt, 0.0).astype(BF16))
        ys.append(yp)
        xws.append((xp * jnp.where(left, wj[:, ha:ha + 1], wj[:, hb:hb + 1])).astype(BF16))
    y = jnp.concatenate(ys, axis=1)
    upd = _dot_tn(jnp.concatenate(xws, axis=1), bm)
    etot = jnp.exp(tot)
    for hd in range(SSM_HPG):
        rows = slice(hd * p, (hd + 1) * p)
        hn = h[rows, :] * etot[:, hd:hd + 1] + upd[rows, :]
        h_scr[rows, :] = hn
        hout_ref[0, 0, rows, :] = hn

    if finish:
        yt = y + yf_ref[0] + dsk_ref[...] * x
        yt = yt * _silu(z_ref[0].astype(F32))
        ms = jnp.mean(yt * yt, axis=-1, keepdims=True)
        y_ref[0] = (yt * lax.rsqrt(ms + RMS_EPS) * ng_ref[...]).astype(y_ref.dtype)
    else:
        y_ref[0] = y


def ssd_scan(xbc, dt, cum, cumt, h0, *, reverse, fin=None):
    b, n, _ = xbc.shape
    q, g, gw, ns, hg = SSM_CHUNK, SSM_GROUPS, SSM_GW, SSM_STATE, SSM_HPG
    nc = n // q

    def ch(k):
        return nc - 1 - k if reverse else k

    in_specs = [
        pl.BlockSpec((1, q, gw), lambda i, j, k: (i, ch(k), j)),
        pl.BlockSpec((1, q, ns), lambda i, j, k: (i, ch(k), D_INNER // ns + j)),
        pl.BlockSpec((1, q, ns), lambda i, j, k: (i, ch(k), D_INNER // ns + g + j)),
        pl.BlockSpec((1, 1, q, hg), lambda i, j, k: (i, j, ch(k), 0)),
        pl.BlockSpec((1, 1, q, hg), lambda i, j, k: (i, j, ch(k), 0)),
        pl.BlockSpec((1, 1, 1, hg, q), lambda i, j, k: (i, j, ch(k), 0, 0)),
        pl.BlockSpec((1, 1, gw, ns), lambda i, j, k: (i, j, 0, 0)),
    ]
    args = [xbc, xbc, xbc, dt, cum, cumt, h0]
    yspec = pl.BlockSpec((1, q, gw), lambda i, j, k: (i, ch(k), j))
    if fin is not None:
        y_other, z, dsk, ng = fin
        in_specs += [yspec, yspec,
                     pl.BlockSpec((1, gw), lambda i, j, k: (0, j)),
                     pl.BlockSpec((1, gw), lambda i, j, k: (0, j))]
        args += [y_other, z, dsk, ng]
    return pl.pallas_call(
        functools.partial(_ssd_scan_kernel, reverse=reverse, finish=fin is not None),
        grid=(b, g, nc),
        in_specs=in_specs,
        out_specs=[yspec, pl.BlockSpec((1, 1, gw, ns), lambda i, j, k: (i, j, 0, 0))],
        out_shape=[jax.ShapeDtypeStruct((b, n, D_INNER), BF16 if fin is not None else F32),
                   jax.ShapeDtypeStruct((b, g, gw, ns), F32)],
        scratch_shapes=[pltpu.VMEM((gw, ns), F32)],
        compiler_params=_cparams(("arbitrary", "arbitrary", "arbitrary")),
        name="ssd_scan_bwd" if reverse else "ssd_scan_fwd",
    )(*args)


def ssd_mixer(u_lat, u_ctx, w_in, conv_w, conv_b, dt_bias, a_log, d_skip, norm_g, ctx_out):
    g, hg = SSM_GROUPS, SSM_HPG
    w_z = w_in[:, :D_INNER].astype(BF16)
    w_xbc = w_in[:, D_INNER:D_INNER + SSM_XBC].astype(BF16)
    w_dt = w_in[:, D_INNER + SSM_XBC:].astype(BF16)
    dsk = jnp.repeat(d_skip.astype(F32), SSM_HEAD_DIM).reshape(1, D_INNER)
    ng = norm_g.astype(F32).reshape(1, D_INNER)

    def project(u):
        b, n, _ = u.shape
        z = matmul(u, w_z, BF16, 512)
        xbc = dwconv(matmul(u, w_xbc, BF16, 512), conv_w, conv_b, 0, SSM_XBC, act=True)
        dt, cum, cumt = ssd_dtprep(matmul(u, w_dt, F32, 2 * SSM_HEADS), dt_bias, a_log)
        nc = n // SSM_CHUNK
        dt = jnp.transpose(dt.reshape(b, n, 2, g, hg), (2, 0, 3, 1, 4))
        cum = jnp.transpose(cum.reshape(b, n, 2, g, hg), (2, 0, 3, 1, 4))
        cumt = jnp.transpose(cumt.reshape(b, nc, 2, g, hg, SSM_CHUNK), (2, 0, 3, 1, 4, 5))
        return z, xbc, dt, cum, cumt

    def bidir(proj, h_f, h_b, want_y):
        z, xbc, dt, cum, cumt = proj
        y_f, s_f = ssd_scan(xbc, dt[0], cum[0], cumt[0], h_f, reverse=False)
        fin = (y_f, z, dsk, ng) if want_y else None
        y, s_b = ssd_scan(xbc, dt[1], cum[1], cumt[1], h_b, reverse=True, fin=fin)
        return y, s_f, s_b

    h0 = jnp.zeros((u_lat.shape[0], g, SSM_GW, SSM_STATE), F32)
    yc, hc_f, hc_b = bidir(project(u_ctx), h0, h0, ctx_out)
    yl, _, _ = bidir(project(u_lat), hc_f, hc_b, True)
    return yl, (yc if ctx_out else None)


def _rope_kernel(x_ref, cos_ref, sin_ref, o_ref):
    cos = cos_ref[...]
    sin = sin_ref[...]
    half = ATTN_HEAD_DIM // 2
    first = (lax.broadcasted_iota(jnp.int32, cos.shape, 1) % ATTN_HEAD_DIM) < half
    for t in range(x_ref.shape[2] // LANE):
        cols = slice(t * LANE, (t + 1) * LANE)
        x = x_ref[0, :, cols].astype(F32)
        partner = jnp.where(first, pltpu.roll(x, LANE - half, axis=1), pltpu.roll(x, half, axis=1))
        o_ref[0, :, cols] = (x * cos + partner * sin).astype(o_ref.dtype)


def rope(qkv, cos_t, sin_t):
    b, n, _ = qkv.shape
    w = ATTN_Q_W + ATTN_KV_W
    tm = 512
    return pl.pallas_call(
        _rope_kernel, grid=(b, n // tm),
        in_specs=[pl.BlockSpec((1, tm, w), lambda i, j: (i, j, 0)),
                  pl.BlockSpec((tm, LANE), lambda i, j: (j, 0)),
                  pl.BlockSpec((tm, LANE), lambda i, j: (j, 0))],
        out_specs=pl.BlockSpec((1, tm, w), lambda i, j: (i, j, 0)),
        out_shape=jax.ShapeDtypeStruct((b, n, w), BF16),
        compiler_params=_cparams(("arbitrary", "arbitrary")),
        name="rope",
    )(qkv, cos_t, sin_t)


def _attn_kernel(*refs, local, seq_len):
    if local:
        (q_ref, k0_ref, k1_ref, k2_ref, v0_ref, v1_ref, v2_ref, kc_ref, vc_ref, sink_ref,
         o_ref, qs_ref) = refs
    else:
        q_ref, kc_ref, vc_ref, sink_ref, o_ref, qs_ref = refs
    blk, dh, grp = ATTN_BLOCK, ATTN_HEAD_DIM, ATTN_GROUP
    rows = grp * blk
    if local:
        n = pl.program_id(1)
        qi = lax.broadcasted_iota(jnp.int32, (rows, 3 * blk), 0) % blk
        sj = lax.broadcasted_iota(jnp.int32, (rows, 3 * blk), 1)
        kpos = (n - 1) * blk + sj
        mask = (jnp.abs(sj - blk - qi) <= WINDOW) & (kpos >= 0) & (kpos < seq_len)
    for kv in range(ATTN_KV_HEADS):
        for g in range(grp):
            c0 = (kv * grp + g) * dh
            qs_ref[g * blk:(g + 1) * blk, :] = q_ref[0, :, c0:c0 + dh]
        qs = qs_ref[...]
        hc = slice(kv * dh, (kv + 1) * dh)
        sink = sink_ref[kv]
        s_ctx = _dot_nt(qs, kc_ref[0, :, hc])
        m = jnp.maximum(jnp.max(s_ctx, axis=-1, keepdims=True), sink)
        if local:
            kl = jnp.concatenate([k0_ref[0, :, hc], k1_ref[0, :, hc], k2_ref[0, :, hc]], axis=0)
            vl = jnp.concatenate([v0_ref[0, :, hc], v1_ref[0, :, hc], v2_ref[0, :, hc]], axis=0)
            s_loc = jnp.where(mask, _dot_nt(qs, kl), -jnp.inf)
            m = jnp.maximum(m, jnp.max(s_loc, axis=-1, keepdims=True))
        p_ctx = jnp.exp(s_ctx - m)
        den = jnp.sum(p_ctx, axis=-1, keepdims=True) + jnp.exp(sink - m)
        o = _dot(p_ctx.astype(BF16), vc_ref[0, :, hc])
        if local:
            p_loc = jnp.exp(s_loc - m)
            den = den + jnp.sum(p_loc, axis=-1, keepdims=True)
            o = o + _dot(p_loc.astype(BF16), vl)
        o = (o / den).astype(o_ref.dtype)
        for g in range(grp):
            c0 = (kv * grp + g) * dh
            o_ref[0, :, c0:c0 + dh] = o[g * blk:(g + 1) * blk, :]


def attention(qk, qkv, qkv_ctx, sink_col, *, local):
    b, n, _ = qk.shape
    nctx = qkv_ctx.shape[1]
    blk = ATTN_BLOCK
    nblk = n // blk
    kcb = ATTN_Q_W // ATTN_KV_W
    vcb = kcb + 1
    qspec = pl.BlockSpec((1, blk, ATTN_Q_W), lambda i, j: (i, j, 0))
    ctx_k = pl.BlockSpec((1, nctx, ATTN_KV_W), lambda i, j: (i, 0, kcb))
    ctx_v = pl.BlockSpec((1, nctx, ATTN_KV_W), lambda i, j: (i, 0, vcb))
    sspec = pl.BlockSpec(sink_col.shape, lambda i, j: (0, 0, 0))
    if local:
        def win(cb, off):
            return pl.BlockSpec((1, blk, ATTN_KV_W),
                                lambda i, j: (i, jnp.clip(j + off, 0, nblk - 1), cb))
        in_specs = [qspec, win(kcb, -1), win(kcb, 0), win(kcb, 1),
                    win(vcb, -1), win(vcb, 0), win(vcb, 1), ctx_k, ctx_v, sspec]
        args = [qk, qk, qk, qk, qkv, qkv, qkv, qkv_ctx, qkv_ctx, sink_col]
    else:
        in_specs = [qspec, ctx_k, ctx_v, sspec]
        args = [qk, qkv_ctx, qkv_ctx, sink_col]
    return pl.pallas_call(
        functools.partial(_attn_kernel, local=local, seq_len=n),
        grid=(b, nblk),
        in_specs=in_specs,
        out_specs=qspec,
        out_shape=jax.ShapeDtypeStruct((b, n, ATTN_Q_W), BF16),
        scratch_shapes=[pltpu.VMEM((ATTN_GROUP * blk, ATTN_HEAD_DIM), BF16)],
        compiler_params=_cparams(("arbitrary", "arbitrary")),
        name="attention_local" if local else "attention_ctx",
    )(*args)


def _rope_tables(n):
    rows = n // GRID_W
    row_id = np.repeat(np.arange(rows, dtype=np.float32), GRID_W)
    col_id = np.tile(np.arange(GRID_W, dtype=np.float32), rows)
    pairs = ATTN_HEAD_DIM // 4
    inv = (np.float32(ROPE_BASE) ** (-np.arange(pairs, dtype=np.float32) / np.float32(pairs))).astype(np.float32)
    ang = np.concatenate([row_id[:, None] * inv, col_id[:, None] * inv], axis=-1).astype(np.float32)
    cos, sin = np.cos(ang.astype(np.float64)), np.sin(ang.astype(np.float64))
    reps = LANE // ATTN_HEAD_DIM
    cos_t = np.tile(np.concatenate([cos, cos], axis=-1), (1, reps))
    sin_t = np.tile(np.concatenate([-sin, sin], axis=-1), (1, reps))
    return jnp.asarray(cos_t, F32), jnp.asarray(sin_t, F32)


def attn_mixer(u_lat, u_ctx, w_qkv, sink, ctx_out):
    dh = ATTN_HEAD_DIM
    perm = np.concatenate([np.arange(0, dh, 2), np.arange(1, dh, 2)])
    nqk = ATTN_Q_HEADS + ATTN_KV_HEADS
    cols = (np.arange(nqk)[:, None] * dh + perm[None, :]).reshape(-1)
    cols = np.concatenate([cols, np.arange(nqk * dh, nqk * dh + ATTN_KV_W)])
    colscale = np.where(np.arange(cols.shape[0]) < ATTN_Q_W, dh ** -0.5, 1.0).astype(np.float32)
    w = (w_qkv[:, cols] * colscale).astype(BF16)
    sink_col = jnp.repeat(sink.astype(F32).reshape(ATTN_KV_HEADS, ATTN_GROUP), ATTN_BLOCK, axis=1)
    sink_col = sink_col.reshape(ATTN_KV_HEADS, ATTN_GROUP * ATTN_BLOCK, 1)

    qkv_l = matmul(u_lat, w, BF16, 512)
    qkv_c = matmul(u_ctx, w, BF16, 512)
    cos_t, sin_t = _rope_tables(u_lat.shape[1])
    qk_l = rope(qkv_l, cos_t, sin_t)
    o_l = attention(qk_l, qkv_l, qkv_c, sink_col, local=True)
    o_c = attention(qkv_c, qkv_c, qkv_c, sink_col, local=False) if ctx_out else None
    return o_l, o_c


def _filter_kernel(z_ref, t_ref, w1_ref, b1_ref, w2_ref, b2_ref, w3_ref, b3_ref, fr_ref,
                   w4f_ref, w4b_ref, dl_ref, o_ref, *, seq_len):
    fr = fr_ref[...]
    h = jnp.sin(fr * (_dot(z_ref[...], w1_ref[...], HIGHEST) + b1_ref[...]))
    h = jnp.sin(fr * (_dot(h, w2_ref[...], HIGHEST) + b2_ref[...]))
    h = jnp.sin(fr * (_dot(h, w3_ref[...], HIGHEST) + b3_ref[...]))
    tr = h.shape[0]
    r = pl.program_id(1) * tr + lax.broadcasted_iota(jnp.int32, (tr, 1), 0)
    hf = _dot(h, w4f_ref[...], HIGHEST)
    hb = _dot(h, w4b_ref[...], HIGHEST)
    val = jnp.where(r < seq_len, hf, hb) * jnp.exp(-t_ref[...] * dl_ref[...])
    o_ref[0] = jnp.where(r == seq_len, 0.0, val)


def hyena_filters(n, w1, b1, w2, b2, w3, b3, w4, freq):
    d = D_MODEL
    fw = HYENA_FILTER_W
    lag = np.concatenate([np.arange(n), [0], np.arange(n - 1, 0, -1)])
    t = np.linspace(0.0, 1.0, n, dtype=np.float32)[:, None]
    bands = (HYENA_EMB - 1) // 2
    wv = (np.float32(2.0 * math.pi) * np.arange(n, dtype=np.float32)[:, None] / np.float32(n)).astype(np.float32)
    f = np.linspace(1e-4, bands - 1, bands, dtype=np.float32)[None, :]
    fwv = (f * wv).astype(np.float32).astype(np.float64)
    z = np.concatenate([t, np.cos(fwv), -np.sin(fwv)], axis=-1).astype(np.float32)
    zpad = np.zeros((2 * n, fw), np.float32)
    zpad[:, :HYENA_EMB] = z[lag]
    tfull = t[lag]
    max_decay = math.log(HYENA_DECAY_TARGET) / HYENA_DECAY_FAST
    min_decay = math.log(HYENA_DECAY_TARGET) / HYENA_DECAY_SLOW
    deltas = np.abs(np.linspace(min_decay, max_decay, d, dtype=np.float32))[None, :]
    w1p = jnp.zeros((fw, fw), F32).at[:HYENA_EMB].set(w1.astype(F32))
    tr = min(2 * n, 512)
    td = 512
    nd = d // td
    par = lambda shape: pl.BlockSpec(shape, lambda o, i, j: (0, 0))
    return pl.pallas_call(
        functools.partial(_filter_kernel, seq_len=n),
        grid=(HYENA_ORDER, 2 * n // tr, nd),
        in_specs=[pl.BlockSpec((tr, fw), lambda o, i, j: (i, 0)),
                  pl.BlockSpec((tr, 1), lambda o, i, j: (i, 0)),
                  par((fw, fw)), par((1, fw)), par((fw, fw)), par((1, fw)),
                  par((fw, fw)), par((1, fw)), par((1, fw)),
                  pl.BlockSpec((fw, td), lambda o, i, j: (0, (2 * o) * nd + j)),
                  pl.BlockSpec((fw, td), lambda o, i, j: (0, (2 * o + 1) * nd + j)),
                  pl.BlockSpec((1, td), lambda o, i, j: (0, j))],
        out_specs=pl.BlockSpec((1, tr, td), lambda o, i, j: (o, i, j)),
        out_shape=jax.ShapeDtypeStruct((HYENA_ORDER, 2 * n, d), F32),
        compiler_params=_cparams(("arbitrary", "arbitrary", "arbitrary")),
        name="hyena_filters",
    )(jnp.asarray(zpad), jnp.asarray(tfull), w1p, b1.astype(F32).reshape(1, fw),
      w2.astype(F32), b2.astype(F32).reshape(1, fw), w3.astype(F32), b3.astype(F32).reshape(1, fw),
      freq.astype(F32).reshape(1, fw), w4.astype(F32), w4.astype(F32), jnp.asarray(deltas))


def _bmm_left_kernel(*refs, gated):
    if gated:
        f_ref, x_ref, g_ref, v_ref, bias_ref, o_ref = refs
    else:
        f_ref, x_ref, o_ref = refs
    y = _dot(f_ref[...], x_ref[0].astype(BF16))
    if gated:
        v = v_ref[0].astype(F32)
        y = g_ref[0].astype(F32) * (y + bias_ref[...] * v)
    o_ref[0] = y.astype(o_ref.dtype)


def bmm_left(f, x, out_dtype, gate=None):
    p, k, c = x.shape
    mo = f.shape[0]
    tc = min(c, 2048)
    xspec = pl.BlockSpec((1, k, tc), lambda i, j: (i, 0, j))
    ospec = pl.BlockSpec((1, mo, tc), lambda i, j: (i, 0, j))
    in_specs = [pl.BlockSpec((mo, k), lambda i, j: (0, 0)), xspec]
    args = [f, x]
    if gate is not None:
        in_specs += [ospec, ospec, pl.BlockSpec((1, tc), lambda i, j: (0, j))]
        args += list(gate)
    return pl.pallas_call(
        functools.partial(_bmm_left_kernel, gated=gate is not None),
        grid=(p, c // tc),
        in_specs=in_specs, out_specs=ospec,
        out_shape=jax.ShapeDtypeStruct((p, mo, c), out_dtype),
        compiler_params=_cparams(("arbitrary", "arbitrary")),
        name="bmm_left_gated" if gate is not None else "bmm_left",
    )(*args)


def _slab_fwd_kernel(m1_ref, t_ref, o_ref):
    r = FFT_SLAB
    t = t_ref[0, :, 0].reshape(2 * r, t_ref.shape[-1]).astype(BF16)
    x = _dot(m1_ref[0], t)
    o_ref[:, 0] = x.reshape(2, r, x.shape[-1])


def slab_spectrum(m1, t):
    _, a, r, d = t.shape
    td = 512
    return pl.pallas_call(
        _slab_fwd_kernel, grid=(a, d // td),
        in_specs=[pl.BlockSpec((1, 2 * r, 2 * r), lambda i, j: (i, 0, 0)),
                  pl.BlockSpec((1, 2, 1, r, td), lambda i, j: (0, 0, i, 0, j))],
        out_specs=pl.BlockSpec((2, 1, r, td), lambda i, j: (0, i, 0, j)),
        out_shape=jax.ShapeDtypeStruct((2, a, r, d), F32),
        compiler_params=_cparams(("arbitrary", "arbitrary")),
        name="slab_spectrum",
    )(m1, t.reshape(1, 2, a, r, d))


def _slab_conv_kernel(m1_ref, m2_ref, h_ref, t_ref, o_ref):
    r = FFT_SLAB
    td = t_ref.shape[-1]
    t = t_ref[0, :, 0].reshape(2 * r, td)
    x = _dot(m1_ref[0], t)
    xr, xi = x[:r], x[r:]
    hr, hi = h_ref[0, 0], h_ref[1, 0]
    y = jnp.concatenate([xr * hr - xi * hi, xr * hi + xi * hr], axis=0).astype(BF16)
    u = _dot(m2_ref[0], y)
    o_ref[0, :, 0] = u.reshape(2, r, td).astype(o_ref.dtype)


def slab_conv(m1, m2, h, t):
    p, _, a, r, d = t.shape
    td = 512
    mspec = pl.BlockSpec((1, 2 * r, 2 * r), lambda i, j, k: (i, 0, 0))
    tspec = pl.BlockSpec((1, 2, 1, r, td), lambda i, j, k: (k, 0, i, 0, j))
    return pl.pallas_call(
        _slab_conv_kernel, grid=(a, d // td, p),
        in_specs=[mspec, mspec, pl.BlockSpec((2, 1, r, td), lambda i, j, k: (0, i, 0, j)), tspec],
        out_specs=tspec,
        out_shape=jax.ShapeDtypeStruct(t.shape, BF16),
        compiler_params=_cparams(("arbitrary", "arbitrary", "arbitrary")),
        name="slab_conv",
    )(m1, m2, h, t)


def _direct_conv_kernel(f1_ref, f2_ref, h_ref, x_ref, g_ref, bias_ref, o_ref):
    xin = x_ref[0]
    half = f1_ref.shape[0] // 2
    x = _dot(f1_ref[...], xin)
    xr, xi = x[:half], x[half:]
    hr, hi = h_ref[0], h_ref[1]
    y = jnp.concatenate([xr * hr - xi * hi, xr * hi + xi * hr], axis=0).astype(BF16)
    u = _dot(f2_ref[...], y)
    o_ref[0] = (g_ref[0].astype(F32) * (u + bias_ref[...] * xin.astype(F32))).astype(o_ref.dtype)


def direct_conv(f1, f2, h, x, g, bias_row):
    p, n2, d = x.shape
    td = 512
    xspec = pl.BlockSpec((1, n2, td), lambda i, j: (i, 0, j))
    return pl.pallas_call(
        _direct_conv_kernel, grid=(p, d // td),
        in_specs=[pl.BlockSpec(f1.shape, lambda i, j: (0, 0)),
                  pl.BlockSpec(f2.shape, lambda i, j: (0, 0)),
                  pl.BlockSpec((2, n2, td), lambda i, j: (0, 0, j)),
                  xspec, xspec, pl.BlockSpec((1, td), lambda i, j: (0, j))],
        out_specs=xspec,
        out_shape=jax.ShapeDtypeStruct((p, n2, d), BF16),
        compiler_params=_cparams(("arbitrary", "arbitrary")),
        name="direct_conv",
    )(f1, f2, h, x, g, bias_row)


def _cplx_real_form(c):
    return np.block([[c.real, -c.imag], [c.imag, c.real]])


def _dft_tables(n):
    nn = 2 * n
    r = FFT_SLAB
    if nn <= 4 * r:
        k = np.arange(nn)[:, None]
        m = np.arange(nn)[None, :]
        fc = np.exp(-2j * np.pi * ((k * m) % nn) / nn)
        f1 = _cplx_real_form(fc[:, :n])
        f2 = _cplx_real_form(np.conj(fc.T)[:n, :] / nn)
        fh = np.concatenate([fc.real, fc.imag], axis=0)
        return dict(f1=jnp.asarray(f1, BF16), f2=jnp.asarray(f2, BF16), fh=jnp.asarray(fh, BF16))
    a = nn // r
    ka = np.arange(a)[:, None]
    aa = np.arange(a)[None, :]
    fa = np.exp(-2j * np.pi * ((ka * aa) % a) / a)
    fa_fwd = _cplx_real_form(fa[:, :a // 2])
    fa_inv = _cplx_real_form(np.conj(fa.T)[:a // 2, :] / nn)
    fa_h = np.concatenate([fa.real, fa.imag], axis=0)
    kb = np.arange(r)[None, :, None]
    bb = np.arange(r)[None, None, :]
    kk = np.arange(a)[:, None, None]
    m1c = np.exp(-2j * np.pi * ((bb * (kk + a * kb)) % nn) / nn)
    m1 = np.stack([_cplx_real_form(m1c[i]) for i in range(a)])
    m2 = np.stack([_cplx_real_form(np.conj(m1c[i].T)) for i in range(a)])
    return dict(fa_fwd=jnp.asarray(fa_fwd, BF16), fa_inv=jnp.asarray(fa_inv, BF16),
                fa_h=jnp.asarray(fa_h, BF16), m1=jnp.asarray(m1, BF16), m2=jnp.asarray(m2, BF16))


def hyena_long_convs(x1, x2, v, hfull, f_bias):
    b, n, d = v.shape
    nn = 2 * n
    r = FFT_SLAB
    tabs = _dft_tables(n)
    p = b // 2
    bias = f_bias.astype(F32)
    if 'f1' in tabs:
        h = bmm_left(tabs['fh'], hfull, F32).reshape(HYENA_ORDER, 2, nn, d)
        pair = lambda t: t.reshape(p, nn, d)
        z = direct_conv(tabs['f1'], tabs['f2'], h[0], pair(v), pair(x1), bias[0].reshape(1, d))
        y = direct_conv(tabs['f1'], tabs['f2'], h[1], z, pair(x2), bias[1].reshape(1, d))
        return y.reshape(b, n, d)
    a = nn // r
    c = r * d
    th = bmm_left(tabs['fa_h'], hfull.reshape(HYENA_ORDER, a, c), BF16)
    h = [slab_spectrum(tabs['m1'], th[o].reshape(2, a, r, d)) for o in range(HYENA_ORDER)]
    pair = lambda t: t.reshape(p, a, c)

    def conv(sig, gate, o):
        t = bmm_left(tabs['fa_fwd'], pair(sig), BF16)
        u = slab_conv(tabs['m1'], tabs['m2'], h[o], t.reshape(p, 2, a, r, d))
        brow = jnp.tile(bias[o], r).reshape(1, c)
        return bmm_left(tabs['fa_inv'], u.reshape(p, 2 * a, c), BF16, gate=(pair(gate), pair(sig), brow))

    z = conv(v, x1, 0)
    y = conv(z.reshape(b, n, d), x2, 1)
    return y.reshape(b, n, d)


def hyena_mixer(u_lat, u_ctx, w_in, conv_w, conv_b, f_w1, f_b1, f_w2, f_b2, f_w3, f_b3, f_w4,
                f_freq, f_bias, ctx_out):
    d = D_MODEL
    w = w_in.astype(BF16)

    def run(u):
        n = u.shape[1]
        hfull = hyena_filters(n, f_w1, f_b1, f_w2, f_b2, f_w3, f_b3, f_w4, f_freq)
        xin = matmul(u, w, BF16, 512)
        x1, x2, v = [dwconv(xin, conv_w[:, i * d:(i + 1) * d], conv_b[i * d:(i + 1) * d], i * d, d, act=False)
                     for i in range(3)]
        return hyena_long_convs(x1, x2, v, hfull, f_bias)

    return run(u_lat), (run(u_ctx) if ctx_out else None)


def kernel(x, c, ctx, c_ctx, ada_w, ada_b, ln_g, ln_b, mlp_w1, mlp_w2, ssd_w_in, ssd_conv_w, ssd_conv_b, ssd_dt_bias, ssd_a_log, ssd_d, ssd_norm_g, ssd_w_out, attn_w_qkv, attn_sink, attn_w_o, hy_w_in, hy_conv_w, hy_conv_b, hy_f_w1, hy_f_b1, hy_f_w2, hy_f_b2, hy_f_w3, hy_f_b3, hy_f_w4, hy_f_freq, hy_f_bias, hy_w_out):
    bsz, _, d = x.shape
    depth = ada_w.shape[0]
    assert bsz + 1 <= COND_ROWS and bsz % 2 == 0
    cond = jnp.zeros((COND_ROWS, d), F32).at[:bsz].set(c.astype(F32)).at[bsz].set(c_ctx.astype(F32))
    mods = ada_mods(cond, ada_w.astype(F32), ada_b.astype(F32))

    def mod_l(i, k):
        return mods[i, :bsz, k * d:(k + 1) * d][:, None, :]

    def mod_c(i, k):
        return jnp.broadcast_to(mods[i, bsz, k * d:(k + 1) * d][None, None, :], (bsz, 1, d))

    zero = jnp.zeros((bsz, 1, d), F32)
    xl, xc = x.astype(F32), ctx.astype(F32)
    ul = modulate(xl, mod_l(0, 1), mod_l(0, 0))
    uc = modulate(xc, mod_c(0, 1), mod_c(0, 0))
    for i in range(depth):
        last = i == depth - 1
        kind = MIXER_OF_LAYER[i]
        j = MIXER_OF_LAYER[:i].count(kind)
        if kind == 0:
            yl, yc = ssd_mixer(ul, uc, ssd_w_in[j], ssd_conv_w[j].astype(F32), ssd_conv_b[j].astype(F32),
                               ssd_dt_bias[j], ssd_a_log[j], ssd_d[j], ssd_norm_g[j], not last)
            w_out = ssd_w_out[j]
        elif kind == 1:
            yl, yc = attn_mixer(ul, uc, attn_w_qkv[j], attn_sink[j], not last)
            w_out = attn_w_o[j]
        else:
            yl, yc = hyena_mixer(ul, uc, hy_w_in[j], hy_conv_w[j].astype(F32), hy_conv_b[j].astype(F32),
                                 hy_f_w1[j], hy_f_b1[j], hy_f_w2[j], hy_f_b2[j], hy_f_w3[j], hy_f_b3[j],
                                 hy_f_w4[j], hy_f_freq[j], hy_f_bias[j], not last)
            w_out = hy_w_out[j]
        w_out = w_out.astype(BF16)
        w1 = mlp_w1[i].astype(BF16)
        w2 = mlp_w2[i].astype(BF16)
        nsc_l, nsh_l = (zero, zero) if last else (mod_l(i + 1, 1), mod_l(i + 1, 0))
        xl, ul = matmul_res_ln(yl, w_out, xl, mod_l(i, 2), ln_g[i, 0], ln_b[i, 0], mod_l(i, 4), mod_l(i, 3))
        xl, ul = mlp_res_ln(ul, w1, w2, xl, mod_l(i, 5), ln_g[i, 1], ln_b[i, 1], nsc_l, nsh_l)
        if not last:
            xc, uc = matmul_res_ln(yc, w_out, xc, mod_c(i, 2), ln_g[i, 0], ln_b[i, 0], mod_c(i, 4), mod_c(i, 3))
            xc, uc = mlp_res_ln(uc, w1, w2, xc, mod_c(i, 5), ln_g[i, 1], ln_b[i, 1],
                                mod_c(i + 1, 1), mod_c(i + 1, 0))
    return xl.astype(x.dtype)
```

```python
import functools
import math

import numpy as np
import jax
import jax.numpy as jnp
from jax import lax
from jax.experimental import pallas as pl
from jax.experimental.pallas import tpu as pltpu

F32 = jnp.float32
BF16 = jnp.bfloat16
HIGHEST = lax.Precision.HIGHEST

D_MODEL = 2048
DEPTH = 4
GRID_W = 64
N_MIXERS = 3
MIXER_OF_LAYER = tuple(i % N_MIXERS for i in range(DEPTH))
ALPHA = (2.0 * DEPTH) ** 0.25
LN_EPS = 1e-5
RMS_EPS = 1e-5
N_MOD = 6
MLP_HIDDEN = 4 * D_MODEL

D_INNER = 2 * D_MODEL
SSM_HEAD_DIM = 64
SSM_HEADS = D_INNER // SSM_HEAD_DIM
SSM_GROUPS = 8
SSM_HPG = SSM_HEADS // SSM_GROUPS
SSM_STATE = 128
SSM_CONV_W = 5
SSM_CHUNK = 128
SSM_XBC = D_INNER + 2 * SSM_GROUPS * SSM_STATE
SSM_GW = D_INNER // SSM_GROUPS
SSM_SPLIT_ROWS = 6 * SSM_HPG
SSM_GPS = 8
LOG2E = math.log2(math.e)

ATTN_HEAD_DIM = 64
ATTN_Q_HEADS = D_MODEL // ATTN_HEAD_DIM
ATTN_KV_HEADS = 4
ATTN_GROUP = ATTN_Q_HEADS // ATTN_KV_HEADS
ATTN_Q_W = ATTN_Q_HEADS * ATTN_HEAD_DIM
ATTN_KV_W = ATTN_KV_HEADS * ATTN_HEAD_DIM
WINDOW = 128
ATTN_BLOCK = 128
ROPE_BASE = 10000.0

HYENA_ORDER = 2
HYENA_SHORT_W = 3
HYENA_EMB = 33
HYENA_FILTER_W = 64
HYENA_DECAY_FAST = 0.3
HYENA_DECAY_SLOW = 1.5
HYENA_DECAY_TARGET = 1e-2

LANE = 128
FFT_SLAB = 128
FFT_SUB = 16
COND_ROWS = 16
VMEM_LIMIT_MB = 56


def _cparams(sem, vmem_mb=VMEM_LIMIT_MB):
    return pltpu.CompilerParams(dimension_semantics=sem,
                                vmem_limit_bytes=vmem_mb * 1024 * 1024)


def _dot(a, b, precision=None):
    return jnp.dot(a, b, preferred_element_type=F32, precision=precision)


def _dot_nt(a, b):
    return lax.dot_general(a, b, (((1,), (1,)), ((), ())), preferred_element_type=F32)


def _dot_tn(a, b):
    return lax.dot_general(a, b, (((0,), (0,)), ((), ())), preferred_element_type=F32)


def _silu(x):
    return x * jax.nn.sigmoid(x)


def _split_bf16(v, parts):
    out = []
    for _ in range(parts):
        piece = v.astype(BF16)
        out.append(piece)
        v = v - piece.astype(F32)
    return out


def _ada_kernel(c_ref, w_ref, b_ref, o_ref):
    s = _silu(c_ref[...])
    o_ref[0] = _dot(s, w_ref[0], HIGHEST) + b_ref[0]


def ada_mods(cond, ada_w, ada_b):
    depth, d, n6 = ada_w.shape
    tn = 512
    return pl.pallas_call(
        _ada_kernel,
        grid=(depth, n6 // tn),
        in_specs=[pl.BlockSpec((COND_ROWS, d), lambda l, j: (0, 0)),
                  pl.BlockSpec((1, d, tn), lambda l, j: (l, 0, j)),
                  pl.BlockSpec((1, 1, tn), lambda l, j: (l, 0, j))],
        out_specs=pl.BlockSpec((1, COND_ROWS, tn), lambda l, j: (l, 0, j)),
        out_shape=jax.ShapeDtypeStruct((depth, COND_ROWS, n6), F32),
        compiler_params=_cparams(("arbitrary", "arbitrary")),
        name="ada_mods",
    )(cond, ada_w, ada_b.reshape(depth, 1, n6))


def _modulate_kernel(x_ref, sc_ref, sh_ref, o_ref):
    o_ref[0] = (x_ref[0] * (1.0 + sc_ref[0]) + sh_ref[0]).astype(o_ref.dtype)


def modulate(x, sc, sh):
    b, n, d = x.shape
    tm = min(n, 1024)
    row = pl.BlockSpec((1, tm, d), lambda i, j: (i, j, 0))
    vec = pl.BlockSpec((1, 1, d), lambda i, j: (i, 0, 0))
    return pl.pallas_call(
        _modulate_kernel, grid=(b, n // tm),
        in_specs=[row, vec, vec], out_specs=row,
        out_shape=jax.ShapeDtypeStruct((b, n, d), BF16),
        compiler_params=_cparams(("arbitrary", "arbitrary")),
        name="modulate",
    )(x, sc, sh)


def _mm_kernel(a_ref, w_ref, o_ref):
    o_ref[0] = _dot(a_ref[0], w_ref[...]).astype(o_ref.dtype)


def matmul(a, w, out_dtype, tn):
    b, n, k = a.shape
    nn = w.shape[1]
    tm = min(n, 1024)
    tn = min(tn, nn)
    return pl.pallas_call(
        _mm_kernel, grid=(b, n // tm, nn // tn),
        in_specs=[pl.BlockSpec((1, tm, k), lambda i, j, l: (i, j, 0)),
                  pl.BlockSpec((k, tn), lambda i, j, l: (0, l))],
        out_specs=pl.BlockSpec((1, tm, tn), lambda i, j, l: (i, j, l)),
        out_shape=jax.ShapeDtypeStruct((b, n, nn), out_dtype),
        compiler_params=_cparams(("arbitrary", "arbitrary", "arbitrary")),
        name="matmul",
    )(a, w)


def _res_ln(x, y, gate, g, b):
    h = ALPHA * x + gate * y
    mu = jnp.mean(h, axis=-1, keepdims=True)
    hc = h - mu
    var = jnp.mean(hc * hc, axis=-1, keepdims=True)
    return hc * lax.rsqrt(var + LN_EPS) * g + b


def _mm_res_ln_kernel(a_ref, w_ref, x_ref, gate_ref, g_ref, b_ref, sc_ref, sh_ref,
                      xo_ref, uo_ref, acc_ref):
    k = pl.program_id(2)

    @pl.when(k == 0)
    def _():
        acc_ref[...] = jnp.zeros_like(acc_ref)

    acc_ref[...] += _dot(a_ref[0], w_ref[...])

    @pl.when(k == pl.num_programs(2) - 1)
    def _():
        xn = _res_ln(x_ref[0], acc_ref[...], gate_ref[0], g_ref[...], b_ref[...])
        xo_ref[0] = xn
        uo_ref[0] = (xn * (1.0 + sc_ref[0]) + sh_ref[0]).astype(uo_ref.dtype)


def matmul_res_ln(a, w, x, gate, ln_g, ln_b, sc_next, sh_next):
    b, n, kk = a.shape
    d = w.shape[1]
    tm = min(n, 512)
    tk = min(kk, 1024)
    row = pl.BlockSpec((1, tm, d), lambda i, j, k: (i, j, 0))
    vec = pl.BlockSpec((1, 1, d), lambda i, j, k: (i, 0, 0))
    par = pl.BlockSpec((1, d), lambda i, j, k: (0, 0))
    return pl.pallas_call(
        _mm_res_ln_kernel, grid=(b, n // tm, kk // tk),
        in_specs=[pl.BlockSpec((1, tm, tk), lambda i, j, k: (i, j, k)),
                  pl.BlockSpec((tk, d), lambda i, j, k: (k, 0)),
                  row, vec, par, par, vec, vec],
        out_specs=[row, row],
        out_shape=[jax.ShapeDtypeStruct((b, n, d), F32), jax.ShapeDtypeStruct((b, n, d), BF16)],
        scratch_shapes=[pltpu.VMEM((tm, d), F32)],
        compiler_params=_cparams(("arbitrary", "arbitrary", "arbitrary")),
        name="matmul_res_ln",
    )(a, w, x, gate, ln_g.reshape(1, d), ln_b.reshape(1, d), sc_next, sh_next)


def _mlp_kernel(u_ref, w1_ref, w2_ref, x_ref, gate_ref, g_ref, b_ref, sc_ref, sh_ref,
                xo_ref, uo_ref, acc_ref):
    k = pl.program_id(2)

    @pl.when(k == 0)
    def _():
        acc_ref[...] = jnp.zeros_like(acc_ref)

    h = jnp.maximum(_dot(u_ref[0], w1_ref[...]), 0.0)
    acc_ref[...] += _dot((h * h).astype(BF16), w2_ref[...])

    @pl.when(k == pl.num_programs(2) - 1)
    def _():
        xn = _res_ln(x_ref[0], acc_ref[...], gate_ref[0], g_ref[...], b_ref[...])
        xo_ref[0] = xn
        uo_ref[0] = (xn * (1.0 + sc_ref[0]) + sh_ref[0]).astype(uo_ref.dtype)


def mlp_res_ln(u, w1, w2, x, gate, ln_g, ln_b, sc_next, sh_next):
    b, n, d = u.shape
    hid = w1.shape[1]
    tm = min(n, 512)
    th = 1024
    row = pl.BlockSpec((1, tm, d), lambda i, j, k: (i, j, 0))
    vec = pl.BlockSpec((1, 1, d), lambda i, j, k: (i, 0, 0))
    par = pl.BlockSpec((1, d), lambda i, j, k: (0, 0))
    return pl.pallas_call(
        _mlp_kernel, grid=(b, n // tm, hid // th),
        in_specs=[row,
                  pl.BlockSpec((d, th), lambda i, j, k: (0, k)),
                  pl.BlockSpec((th, d), lambda i, j, k: (k, 0)),
                  row, vec, par, par, vec, vec],
        out_specs=[row, row],
        out_shape=[jax.ShapeDtypeStruct((b, n, d), F32), jax.ShapeDtypeStruct((b, n, d), BF16)],
        scratch_shapes=[pltpu.VMEM((tm, d), F32)],
        compiler_params=_cparams(("arbitrary", "arbitrary", "arbitrary")),
        name="mlp_res_ln",
    )(u, w1, w2, x, gate, ln_g.reshape(1, d), ln_b.reshape(1, d), sc_next, sh_next)


def _dwconv_kernel(x_ref, w_ref, b_ref, o_ref, *, width, act):
    x = x_ref[0].astype(F32)
    n = x.shape[0]
    half = width // 2
    rows = lax.broadcasted_iota(jnp.int32, x.shape, 0)
    acc = x * w_ref[half:half + 1, :] + b_ref[...]
    for k in range(width):
        off = k - half
        if off == 0:
            continue
        shifted = pltpu.roll(x, (-off) % n, axis=0)
        valid = (rows + off >= 0) & (rows + off < n)
        acc = acc + jnp.where(valid, shifted, 0.0) * w_ref[k:k + 1, :]
    if act:
        acc = _silu(acc)
    o_ref[0] = acc.astype(o_ref.dtype)


def dwconv(x, w, bias, col0, ncols, act, out_dtype=BF16):
    b, n, _ = x.shape
    width = w.shape[0]
    tc = 256 if n > 1024 else 512
    tc = min(tc, ncols)
    cb0 = col0 // tc
    return pl.pallas_call(
        functools.partial(_dwconv_kernel, width=width, act=act),
        grid=(b, ncols // tc),
        in_specs=[pl.BlockSpec((1, n, tc), lambda i, j: (i, 0, cb0 + j)),
                  pl.BlockSpec((width, tc), lambda i, j: (0, j)),
                  pl.BlockSpec((1, tc), lambda i, j: (0, j))],
        out_specs=pl.BlockSpec((1, n, tc), lambda i, j: (i, 0, j)),
        out_shape=jax.ShapeDtypeStruct((b, n, ncols), out_dtype),
        compiler_params=_cparams(("arbitrary", "arbitrary")),
        name="dwconv",
    )(x, w, bias.reshape(1, ncols))


def _dtprep_kernel(raw_ref, bias_ref, a_ref, dt_ref, cum_ref, cumt_ref, ecum_ref, wj_ref, etot_ref):
    x = raw_ref[0] + bias_ref[...]
    dt = jnp.maximum(x, 0.0) + jnp.log1p(jnp.exp(-jnp.abs(x)))
    adt = dt * a_ref[...]
    q = x.shape[0]
    r = lax.broadcasted_iota(jnp.int32, (q, q), 0)
    c = lax.broadcasted_iota(jnp.int32, (q, q), 1)
    fwd = _dot((c <= r).astype(F32), adt, HIGHEST)
    bwd = _dot((c >= r).astype(F32), adt, HIGHEST)
    lane = lax.broadcasted_iota(jnp.int32, x.shape, 1)
    cum = jnp.where(lane < SSM_HEADS, fwd, bwd)
    tot = jnp.where(lane[0:1] < SSM_HEADS, cum[q - 1:q], cum[0:1])
    cum2 = cum * LOG2E
    dt_ref[0] = dt
    cum_ref[0] = cum2
    cumt_ref[0] = cum2.T
    ecum_ref[0] = jnp.exp(cum)
    wj_ref[0] = jnp.exp(tot - cum) * dt
    etot_ref[0, 0] = jnp.exp(tot)


def ssd_dtprep(dt_raw, dt_bias, a_log):
    b, n, w = dt_raw.shape
    q = SSM_CHUNK
    nc = n // q
    a = -jnp.exp(a_log.astype(F32)).reshape(1, w)
    blk = pl.BlockSpec((1, q, w), lambda i, j: (i, j, 0))
    par = pl.BlockSpec((1, w), lambda i, j: (0, 0))
    full = jax.ShapeDtypeStruct((b, n, w), F32)
    return pl.pallas_call(
        _dtprep_kernel, grid=(b, nc),
        in_specs=[blk, par, par],
        out_specs=[blk, blk, pl.BlockSpec((1, w, q), lambda i, j: (i, j, 0)), blk, blk,
                   pl.BlockSpec((1, 1, 1, w), lambda i, j: (i, j, 0, 0))],
        out_shape=[full, full, jax.ShapeDtypeStruct((b, nc * w, q), F32), full, full,
                   jax.ShapeDtypeStruct((b, nc, 1, w), F32)],
        compiler_params=_cparams(("arbitrary", "arbitrary")),
        name="ssd_dtprep",
    )(dt_raw, dt_bias.reshape(1, w).astype(F32), a)


def _ssd_scan_kernel(*refs, reverse, finish):
    if finish:
        (x_ref, b_ref, c_ref, a_ref, r0_ref, f_ref, bmask_ref, eexp_ref, etot_ref, h0_ref,
         yf_ref, z_ref, dsk_ref, ng_ref, y_ref, hout_ref, h_scr) = refs
    else:
        (x_ref, b_ref, c_ref, a_ref, r0_ref, f_ref, bmask_ref, eexp_ref, etot_ref, h0_ref,
         y_ref, hout_ref, h_scr) = refs
    step = pl.program_id(2)

    @pl.when(step == 0)
    def _():
        h_scr[...] = h0_ref[0]

    q = SSM_CHUNK
    p = SSM_HEAD_DIM
    gw = SSM_GW
    ns = SSM_STATE
    ri = lax.broadcasted_iota(jnp.int32, (q, q), 0)
    ci = lax.broadcasted_iota(jnp.int32, (q, q), 1)
    keep = (ci >= ri) if reverse else (ci <= ri)
    left = lax.broadcasted_iota(jnp.int32, (q, 2 * p), 1) < p

    for gi in range(SSM_GPS):
        gcols = slice(gi * gw, (gi + 1) * gw)
        x = x_ref[0, :, gcols].astype(F32)
        bm = b_ref[0, :, gi * ns:(gi + 1) * ns]
        cm = c_ref[0, :, gi * ns:(gi + 1) * ns]
        cb = jnp.where(keep, _dot_nt(cm, bm), 0.0)
        rhs = jnp.concatenate([r0_ref[0, gi, 0]] * SSM_HPG, axis=1) * bmask_ref[...]
        diff = _dot(a_ref[0, gi], rhs)
        fac = _dot(f_ref[0, gi], eexp_ref[...])
        xdt = x * fac[:, :gw]
        xw = (x * fac[:, 2 * gw:]).astype(BF16)
        ht = h_scr[gi]
        yoff = _dot(cm, ht.astype(BF16)) * fac[:, gw:2 * gw]

        ys = []
        for pr in range(SSM_HPG // 2):
            cols = slice(pr * 2 * p, (pr + 1) * 2 * p)
            xp = xdt[:, cols]
            yp = yoff[:, cols]
            for hd, xm in ((2 * pr, jnp.where(left, xp, 0.0)), (2 * pr + 1, jnp.where(left, 0.0, xp))):
                dec = jnp.exp2(jnp.minimum(diff[:, hd * q:(hd + 1) * q], 0.0))
                yp = yp + _dot((cb * dec).astype(BF16), xm.astype(BF16))
            ys.append(yp)
        y = jnp.concatenate(ys, axis=1)
        h_scr[gi] = ht * etot_ref[0, gi, 0] + _dot_tn(bm, xw)

        if finish:
            yt = y + yf_ref[0, :, gcols] + dsk_ref[:, gcols] * x
            yt = yt * _silu(z_ref[0, :, gcols].astype(F32))
            ms = jnp.mean(yt * yt, axis=-1, keepdims=True)
            y_ref[0, :, gcols] = (yt * lax.rsqrt(ms + RMS_EPS) * ng_ref[:, gcols]).astype(y_ref.dtype)
        else:
            y_ref[0, :, gcols] = y

    @pl.when(step == pl.num_programs(2) - 1)
    def _():
        hout_ref[0] = h_scr[...]


def _ssd_tables():
    q, hg, p = SSM_CHUNK, SSM_HPG, SSM_HEAD_DIM
    r = np.arange(SSM_SPLIT_ROWS)[:, None]
    bmask = (r % hg == np.arange(hg * q)[None, :] // q)
    c = np.arange(3 * SSM_GW)[None, :]
    eexp = (r // (2 * hg) == c // SSM_GW) & (r % hg == (c % SSM_GW) // p)
    return jnp.asarray(bmask, BF16), jnp.asarray(eexp, BF16)


def ssd_scan(xbc, a, r0, f, etot, h0, *, reverse, fin=None):
    b, n, _ = xbc.shape
    q, g, gw, ns = SSM_CHUNK, SSM_GROUPS, SSM_GW, SSM_STATE
    sr = SSM_SPLIT_ROWS
    nc = n // q
    bmask, eexp = _ssd_tables()

    def ch(k):
        return nc - 1 - k if reverse else k

    gps = SSM_GPS
    bcol = D_INNER // (gps * ns)
    in_specs = [
        pl.BlockSpec((1, q, gps * gw), lambda i, j, k: (i, ch(k), j)),
        pl.BlockSpec((1, q, gps * ns), lambda i, j, k: (i, ch(k), bcol + j)),
        pl.BlockSpec((1, q, gps * ns), lambda i, j, k: (i, ch(k), bcol + g // gps + j)),
        pl.BlockSpec((1, gps, q, sr), lambda i, j, k: (i, j, ch(k), 0)),
        pl.BlockSpec((1, gps, 1, sr, q), lambda i, j, k: (i, j, ch(k), 0, 0)),
        pl.BlockSpec((1, gps, q, sr), lambda i, j, k: (i, j, ch(k), 0)),
        pl.BlockSpec(bmask.shape, lambda i, j, k: (0, 0)),
        pl.BlockSpec(eexp.shape, lambda i, j, k: (0, 0)),
        pl.BlockSpec((1, gps, 1, 1, gw), lambda i, j, k: (i, j, ch(k), 0, 0)),
        pl.BlockSpec((1, gps, ns, gw), lambda i, j, k: (i, j, 0, 0)),
    ]
    args = [xbc, xbc, xbc, a, r0, f, bmask, eexp, etot, h0]
    yspec = pl.BlockSpec((1, q, gps * gw), lambda i, j, k: (i, ch(k), j))
    if fin is not None:
        y_other, z, dsk, ng = fin
        in_specs += [yspec, yspec,
                     pl.BlockSpec((1, gps * gw), lambda i, j, k: (0, j)),
                     pl.BlockSpec((1, gps * gw), lambda i, j, k: (0, j))]
        args += [y_other, z, dsk, ng]
    return pl.pallas_call(
        functools.partial(_ssd_scan_kernel, reverse=reverse, finish=fin is not None),
        grid=(b, g // gps, nc),
        in_specs=in_specs,
        out_specs=[yspec, pl.BlockSpec((1, gps, ns, gw), lambda i, j, k: (i, j, 0, 0))],
        out_shape=[jax.ShapeDtypeStruct((b, n, D_INNER), BF16 if fin is not None else F32),
                   jax.ShapeDtypeStruct((b, g, ns, gw), F32)],
        scratch_shapes=[pltpu.VMEM((gps, ns, gw), F32)],
        compiler_params=_cparams(("arbitrary", "arbitrary", "arbitrary")),
        name="ssd_scan_bwd" if reverse else "ssd_scan_fwd",
    )(*args)


def ssd_mixer(u_lat, u_ctx, w_in, conv_w, conv_b, dt_bias, a_log, d_skip, norm_g, ctx_out):
    g, hg = SSM_GROUPS, SSM_HPG
    w_z = w_in[:, :D_INNER].astype(BF16)
    w_xbc = w_in[:, D_INNER:D_INNER + SSM_XBC].astype(BF16)
    w_dt = w_in[:, D_INNER + SSM_XBC:].astype(BF16)
    dsk = jnp.repeat(d_skip.astype(F32), SSM_HEAD_DIM).reshape(1, D_INNER)
    ng = norm_g.astype(F32).reshape(1, D_INNER)

    def project(u):
        b, n, _ = u.shape
        z = matmul(u, w_z, BF16, 512)
        xbc = dwconv(matmul(u, w_xbc, BF16, 512), conv_w, conv_b, 0, SSM_XBC, act=True)
        dt, cum, cumt, ecum, wj, etot = ssd_dtprep(matmul(u, w_dt, F32, 2 * SSM_HEADS), dt_bias, a_log)
        nc = n // SSM_CHUNK
        q = SSM_CHUNK

        def rows(v):
            return jnp.transpose(v.reshape(b, n, 2, g, hg), (2, 0, 3, 1, 4))

        a = jnp.concatenate([rows(s) for s in _split_bf16(cum, 3)]
                            + [jnp.ones((2, b, g, n, 3 * hg), BF16)], axis=-1)
        f = jnp.concatenate([rows(s) for v in (dt, ecum, wj) for s in _split_bf16(v, 2)], axis=-1)
        cumt = jnp.transpose(cumt.reshape(b, nc, 2, g, hg, q), (2, 0, 3, 1, 4, 5))
        r0 = jnp.concatenate([jnp.ones((2, b, g, nc, 3 * hg, q), BF16)]
                             + [-s for s in _split_bf16(cumt, 3)], axis=-2)
        etot = jnp.transpose(etot.reshape(b, nc, 2, g, hg), (2, 0, 3, 1, 4))
        etot = jnp.repeat(etot, SSM_HEAD_DIM, axis=-1)[..., None, :]
        return z, xbc, a, r0, f, etot

    def bidir(proj, h_f, h_b, want_y):
        z, xbc, a, r0, f, etot = proj
        y_f, s_f = ssd_scan(xbc, a[0], r0[0], f[0], etot[0], h_f, reverse=False)
        fin = (y_f, z, dsk, ng) if want_y else None
        y, s_b = ssd_scan(xbc, a[1], r0[1], f[1], etot[1], h_b, reverse=True, fin=fin)
        return y, s_f, s_b

    h0 = jnp.zeros((u_lat.shape[0], g, SSM_STATE, SSM_GW), F32)
    yc, hc_f, hc_b = bidir(project(u_ctx), h0, h0, ctx_out)
    yl, _, _ = bidir(project(u_lat), hc_f, hc_b, True)
    return yl, (yc if ctx_out else None)


def _rope_kernel(x_ref, cos_ref, sin_ref, o_ref):
    cos = cos_ref[...]
    sin = sin_ref[...]
    half = ATTN_HEAD_DIM // 2
    first = (lax.broadcasted_iota(jnp.int32, cos.shape, 1) % ATTN_HEAD_DIM) < half
    for t in range(x_ref.shape[2] // LANE):
        cols = slice(t * LANE, (t + 1) * LANE)
        x = x_ref[0, :, cols].astype(F32)
        partner = jnp.where(first, pltpu.roll(x, LANE - half, axis=1), pltpu.roll(x, half, axis=1))
        o_ref[0, :, cols] = (x * cos + partner * sin).astype(o_ref.dtype)


def rope(qkv, cos_t, sin_t):
    b, n, _ = qkv.shape
    w = ATTN_Q_W + ATTN_KV_W
    tm = 512
    return pl.pallas_call(
        _rope_kernel, grid=(b, n // tm),
        in_specs=[pl.BlockSpec((1, tm, w), lambda i, j: (i, j, 0)),
                  pl.BlockSpec((tm, LANE), lambda i, j: (j, 0)),
                  pl.BlockSpec((tm, LANE), lambda i, j: (j, 0))],
        out_specs=pl.BlockSpec((1, tm, w), lambda i, j: (i, j, 0)),
        out_shape=jax.ShapeDtypeStruct((b, n, w), BF16),
        compiler_params=_cparams(("arbitrary", "arbitrary")),
        name="rope",
    )(qkv, cos_t, sin_t)


def _attn_kernel(*refs, local, seq_len):
    if local:
        (q_ref, k0_ref, k1_ref, k2_ref, v0_ref, v1_ref, v2_ref, kc_ref, vc_ref, sink_ref,
         o_ref, qs_ref) = refs
    else:
        q_ref, kc_ref, vc_ref, sink_ref, o_ref, qs_ref = refs
    blk, dh, grp = ATTN_BLOCK, ATTN_HEAD_DIM, ATTN_GROUP
    rows = grp * blk
    if local:
        n = pl.program_id(1)
        qi = lax.broadcasted_iota(jnp.int32, (rows, 3 * blk), 0) % blk
        sj = lax.broadcasted_iota(jnp.int32, (rows, 3 * blk), 1)
        kpos = (n - 1) * blk + sj
        mask = (jnp.abs(sj - blk - qi) <= WINDOW) & (kpos >= 0) & (kpos < seq_len)
    for kv in range(ATTN_KV_HEADS):
        for g in range(grp):
            c0 = (kv * grp + g) * dh
            qs_ref[g * blk:(g + 1) * blk, :] = q_ref[0, :, c0:c0 + dh]
        qs = qs_ref[...]
        hc = slice(kv * dh, (kv + 1) * dh)
        sink = sink_ref[kv]
        s_ctx = _dot_nt(qs, kc_ref[0, :, hc])
        m = jnp.maximum(jnp.max(s_ctx, axis=-1, keepdims=True), sink)
        if local:
            kl = jnp.concatenate([k0_ref[0, :, hc], k1_ref[0, :, hc], k2_ref[0, :, hc]], axis=0)
            vl = jnp.concatenate([v0_ref[0, :, hc], v1_ref[0, :, hc], v2_ref[0, :, hc]], axis=0)
            s_loc = jnp.where(mask, _dot_nt(qs, kl), -jnp.inf)
            m = jnp.maximum(m, jnp.max(s_loc, axis=-1, keepdims=True))
        p_ctx = jnp.exp(s_ctx - m)
        den = jnp.sum(p_ctx, axis=-1, keepdims=True) + jnp.exp(sink - m)
        o = _dot(p_ctx.astype(BF16), vc_ref[0, :, hc])
        if local:
            p_loc = jnp.exp(s_loc - m)
            den = den + jnp.sum(p_loc, axis=-1, keepdims=True)
            o = o + _dot(p_loc.astype(BF16), vl)
        o = (o / den).astype(o_ref.dtype)
        for g in range(grp):
            c0 = (kv * grp + g) * dh
            o_ref[0, :, c0:c0 + dh] = o[g * blk:(g + 1) * blk, :]


def attention(qk, qkv, qkv_ctx, sink_col, *, local):
    b, n, _ = qk.shape
    nctx = qkv_ctx.shape[1]
    blk = ATTN_BLOCK
    nblk = n // blk
    kcb = ATTN_Q_W // ATTN_KV_W
    vcb = kcb + 1
    qspec = pl.BlockSpec((1, blk, ATTN_Q_W), lambda i, j: (i, j, 0))
    ctx_k = pl.BlockSpec((1, nctx, ATTN_KV_W), lambda i, j: (i, 0, kcb))
    ctx_v = pl.BlockSpec((1, nctx, ATTN_KV_W), lambda i, j: (i, 0, vcb))
    sspec = pl.BlockSpec(sink_col.shape, lambda i, j: (0, 0, 0))
    if local:
        def win(cb, off):
            return pl.BlockSpec((1, blk, ATTN_KV_W),
                                lambda i, j: (i, jnp.clip(j + off, 0, nblk - 1), cb))
        in_specs = [qspec, win(kcb, -1), win(kcb, 0), win(kcb, 1),
                    win(vcb, -1), win(vcb, 0), win(vcb, 1), ctx_k, ctx_v, sspec]
        args = [qk, qk, qk, qk, qkv, qkv, qkv, qkv_ctx, qkv_ctx, sink_col]
    else:
        in_specs = [qspec, ctx_k, ctx_v, sspec]
        args = [qk, qkv_ctx, qkv_ctx, sink_col]
    return pl.pallas_call(
        functools.partial(_attn_kernel, local=local, seq_len=n),
        grid=(b, nblk),
        in_specs=in_specs,
        out_specs=qspec,
        out_shape=jax.ShapeDtypeStruct((b, n, ATTN_Q_W), BF16),
        scratch_shapes=[pltpu.VMEM((ATTN_GROUP * blk, ATTN_HEAD_DIM), BF16)],
        compiler_params=_cparams(("arbitrary", "arbitrary")),
        name="attention_local" if local else "attention_ctx",
    )(*args)


def _rope_tables(n):
    rows = n // GRID_W
    row_id = np.repeat(np.arange(rows, dtype=np.float32), GRID_W)
    col_id = np.tile(np.arange(GRID_W, dtype=np.float32), rows)
    pairs = ATTN_HEAD_DIM // 4
    inv = (np.float32(ROPE_BASE) ** (-np.arange(pairs, dtype=np.float32) / np.float32(pairs))).astype(np.float32)
    ang = np.concatenate([row_id[:, None] * inv, col_id[:, None] * inv], axis=-1).astype(np.float32)
    cos, sin = np.cos(ang.astype(np.float64)), np.sin(ang.astype(np.float64))
    reps = LANE // ATTN_HEAD_DIM
    cos_t = np.tile(np.concatenate([cos, cos], axis=-1), (1, reps))
    sin_t = np.tile(np.concatenate([-sin, sin], axis=-1), (1, reps))
    return jnp.asarray(cos_t, F32), jnp.asarray(sin_t, F32)


def attn_mixer(u_lat, u_ctx, w_qkv, sink, ctx_out):
    dh = ATTN_HEAD_DIM
    perm = np.concatenate([np.arange(0, dh, 2), np.arange(1, dh, 2)])
    nqk = ATTN_Q_HEADS + ATTN_KV_HEADS
    cols = (np.arange(nqk)[:, None] * dh + perm[None, :]).reshape(-1)
    cols = np.concatenate([cols, np.arange(nqk * dh, nqk * dh + ATTN_KV_W)])
    colscale = np.where(np.arange(cols.shape[0]) < ATTN_Q_W, dh ** -0.5, 1.0).astype(np.float32)
    w = (w_qkv[:, cols] * colscale).astype(BF16)
    sink_col = jnp.repeat(sink.astype(F32).reshape(ATTN_KV_HEADS, ATTN_GROUP), ATTN_BLOCK, axis=1)
    sink_col = sink_col.reshape(ATTN_KV_HEADS, ATTN_GROUP * ATTN_BLOCK, 1)

    qkv_l = matmul(u_lat, w, BF16, 512)
    qkv_c = matmul(u_ctx, w, BF16, 512)
    cos_t, sin_t = _rope_tables(u_lat.shape[1])
    qk_l = rope(qkv_l, cos_t, sin_t)
    o_l = attention(qk_l, qkv_l, qkv_c, sink_col, local=True)
    o_c = attention(qkv_c, qkv_c, qkv_c, sink_col, local=False) if ctx_out else None
    return o_l, o_c


def _filter_kernel(z_ref, t_ref, w1_ref, b1_ref, w2_ref, b2_ref, w3_ref, b3_ref, fr_ref,
                   w4f_ref, w4b_ref, dl_ref, o_ref, *, seq_len):
    fr = fr_ref[...]
    h = jnp.sin(fr * (_dot(z_ref[...], w1_ref[...], HIGHEST) + b1_ref[...]))
    h = jnp.sin(fr * (_dot(h, w2_ref[...], HIGHEST) + b2_ref[...]))
    h = jnp.sin(fr * (_dot(h, w3_ref[...], HIGHEST) + b3_ref[...]))
    tr = h.shape[0]
    r = pl.program_id(1) * tr + lax.broadcasted_iota(jnp.int32, (tr, 1), 0)
    hf = _dot(h, w4f_ref[...], HIGHEST)
    hb = _dot(h, w4b_ref[...], HIGHEST)
    val = jnp.where(r < seq_len, hf, hb) * jnp.exp(-t_ref[...] * dl_ref[...])
    o_ref[0] = jnp.where(r == seq_len, 0.0, val)


def hyena_filters(n, w1, b1, w2, b2, w3, b3, w4, freq):
    d = D_MODEL
    fw = HYENA_FILTER_W
    lag = np.concatenate([np.arange(n), [0], np.arange(n - 1, 0, -1)])
    t = np.linspace(0.0, 1.0, n, dtype=np.float32)[:, None]
    bands = (HYENA_EMB - 1) // 2
    wv = (np.float32(2.0 * math.pi) * np.arange(n, dtype=np.float32)[:, None] / np.float32(n)).astype(np.float32)
    f = np.linspace(1e-4, bands - 1, bands, dtype=np.float32)[None, :]
    fwv = (f * wv).astype(np.float32).astype(np.float64)
    z = np.concatenate([t, np.cos(fwv), -np.sin(fwv)], axis=-1).astype(np.float32)
    zpad = np.zeros((2 * n, fw), np.float32)
    zpad[:, :HYENA_EMB] = z[lag]
    tfull = t[lag]
    max_decay = math.log(HYENA_DECAY_TARGET) / HYENA_DECAY_FAST
    min_decay = math.log(HYENA_DECAY_TARGET) / HYENA_DECAY_SLOW
    deltas = np.abs(np.linspace(min_decay, max_decay, d, dtype=np.float32))[None, :]
    w1p = jnp.zeros((fw, fw), F32).at[:HYENA_EMB].set(w1.astype(F32))
    tr = min(2 * n, 512)
    td = 512
    nd = d // td
    par = lambda shape: pl.BlockSpec(shape, lambda o, i, j: (0, 0))
    return pl.pallas_call(
        functools.partial(_filter_kernel, seq_len=n),
        grid=(HYENA_ORDER, 2 * n // tr, nd),
        in_specs=[pl.BlockSpec((tr, fw), lambda o, i, j: (i, 0)),
                  pl.BlockSpec((tr, 1), lambda o, i, j: (i, 0)),
                  par((fw, fw)), par((1, fw)), par((fw, fw)), par((1, fw)),
                  par((fw, fw)), par((1, fw)), par((1, fw)),
                  pl.BlockSpec((fw, td), lambda o, i, j: (0, (2 * o) * nd + j)),
                  pl.BlockSpec((fw, td), lambda o, i, j: (0, (2 * o + 1) * nd + j)),
                  pl.BlockSpec((1, td), lambda o, i, j: (0, j))],
        out_specs=pl.BlockSpec((1, tr, td), lambda o, i, j: (o, i, j)),
        out_shape=jax.ShapeDtypeStruct((HYENA_ORDER, 2 * n, d), F32),
        compiler_params=_cparams(("arbitrary", "arbitrary", "arbitrary")),
        name="hyena_filters",
    )(jnp.asarray(zpad), jnp.asarray(tfull), w1p, b1.astype(F32).reshape(1, fw),
      w2.astype(F32), b2.astype(F32).reshape(1, fw), w3.astype(F32), b3.astype(F32).reshape(1, fw),
      freq.astype(F32).reshape(1, fw), w4.astype(F32), w4.astype(F32), jnp.asarray(deltas))


def _bmm_left_kernel(*refs, gated):
    if gated:
        f_ref, x_ref, g_ref, v_ref, bias_ref, o_ref = refs
    else:
        f_ref, x_ref, o_ref = refs
    y = _dot(f_ref[...], x_ref[0].astype(BF16))
    if gated:
        v = v_ref[0].astype(F32)
        y = g_ref[0].astype(F32) * (y + bias_ref[...] * v)
    o_ref[0] = y.astype(o_ref.dtype)


def bmm_left(f, x, out_dtype, gate=None):
    p, k, c = x.shape
    mo = f.shape[0]
    tc = min(c, 2048)
    xspec = pl.BlockSpec((1, k, tc), lambda i, j: (i, 0, j))
    ospec = pl.BlockSpec((1, mo, tc), lambda i, j: (i, 0, j))
    in_specs = [pl.BlockSpec((mo, k), lambda i, j: (0, 0)), xspec]
    args = [f, x]
    if gate is not None:
        in_specs += [ospec, ospec, pl.BlockSpec((1, tc), lambda i, j: (0, j))]
        args += list(gate)
    return pl.pallas_call(
        functools.partial(_bmm_left_kernel, gated=gate is not None),
        grid=(p, c // tc),
        in_specs=in_specs, out_specs=ospec,
        out_shape=jax.ShapeDtypeStruct((p, mo, c), out_dtype),
        compiler_params=_cparams(("arbitrary", "arbitrary")),
        name="bmm_left_gated" if gate is not None else "bmm_left",
    )(*args)


def _stride_dft_kernel(*refs, gated):
    if gated:
        f_ref, x_ref, g_ref, v_ref, bias_ref, o_ref = refs
    else:
        f_ref, x_ref, o_ref = refs
    td = x_ref.shape[-1]
    x = x_ref[0].reshape(-1, td).astype(BF16)
    y = _dot(f_ref[...], x).reshape(o_ref.shape[1:])
    if gated:
        y = g_ref[0].astype(F32) * (y + bias_ref[...] * v_ref[0].astype(F32))
    o_ref[0] = y.astype(o_ref.dtype)


def stride_dft(fk, x, out_dtype, gate=None):
    p, k, r, d = x.shape
    sub = FFT_SUB
    mo = fk.shape[0] // sub
    td = min(d, 1024)
    xspec = pl.BlockSpec((1, k, sub, td), lambda i, j, l: (i, 0, j, l))
    ospec = pl.BlockSpec((1, mo, sub, td), lambda i, j, l: (i, 0, j, l))
    in_specs = [pl.BlockSpec(fk.shape, lambda i, j, l: (0, 0)), xspec]
    args = [fk, x]
    if gate is not None:
        in_specs += [ospec, ospec, pl.BlockSpec((1, td), lambda i, j, l: (0, l))]
        args += list(gate)
    return pl.pallas_call(
        functools.partial(_stride_dft_kernel, gated=gate is not None),
        grid=(p, r // sub, d // td),
        in_specs=in_specs, out_specs=ospec,
        out_shape=jax.ShapeDtypeStruct((p, mo, r, d), out_dtype),
        compiler_params=_cparams(("arbitrary", "arbitrary", "arbitrary")),
        name="stride_dft_gated" if gate is not None else "stride_dft",
    )(*args)


def _slab_fwd_kernel(m1_ref, t_ref, o_ref):
    r = FFT_SLAB
    t = t_ref[0, :, 0].reshape(2 * r, t_ref.shape[-1]).astype(BF16)
    x = _dot(m1_ref[0], t)
    o_ref[:, 0] = x.reshape(2, r, x.shape[-1])


def slab_spectrum(m1, t):
    _, a, r, d = t.shape
    td = min(d, 2048)
    return pl.pallas_call(
        _slab_fwd_kernel, grid=(a, d // td),
        in_specs=[pl.BlockSpec((1, 2 * r, 2 * r), lambda i, j: (i, 0, 0)),
                  pl.BlockSpec((1, 2, 1, r, td), lambda i, j: (0, 0, i, 0, j))],
        out_specs=pl.BlockSpec((2, 1, r, td), lambda i, j: (0, i, 0, j)),
        out_shape=jax.ShapeDtypeStruct((2, a, r, d), F32),
        compiler_params=_cparams(("arbitrary", "arbitrary")),
        name="slab_spectrum",
    )(m1, t.reshape(1, 2, a, r, d))


def _slab_conv_kernel(m1_ref, m2_ref, h_ref, t_ref, o_ref):
    r = FFT_SLAB
    td = t_ref.shape[-1]
    t = t_ref[0, :, 0].reshape(2 * r, td)
    x = _dot(m1_ref[0], t)
    xr, xi = x[:r], x[r:]
    hr, hi = h_ref[0, 0], h_ref[1, 0]
    y = jnp.concatenate([xr * hr - xi * hi, xr * hi + xi * hr], axis=0).astype(BF16)
    u = _dot(m2_ref[0], y)
    o_ref[0, :, 0] = u.reshape(2, r, td).astype(o_ref.dtype)


def slab_conv(m1, m2, h, t):
    p, _, a, r, d = t.shape
    td = min(d, 2048)
    mspec = pl.BlockSpec((1, 2 * r, 2 * r), lambda i, j, k: (i, 0, 0))
    tspec = pl.BlockSpec((1, 2, 1, r, td), lambda i, j, k: (k, 0, i, 0, j))
    return pl.pallas_call(
        _slab_conv_kernel, grid=(a, d // td, p),
        in_specs=[mspec, mspec, pl.BlockSpec((2, 1, r, td), lambda i, j, k: (0, i, 0, j)), tspec],
        out_specs=tspec,
        out_shape=jax.ShapeDtypeStruct(t.shape, BF16),
        compiler_params=_cparams(("arbitrary", "arbitrary", "arbitrary")),
        name="slab_conv",
    )(m1, m2, h, t)


def _direct_conv_kernel(f1_ref, f2_ref, h_ref, x_ref, g_ref, bias_ref, o_ref):
    xin = x_ref[0]
    half = f1_ref.shape[0] // 2
    x = _dot(f1_ref[...], xin)
    xr, xi = x[:half], x[half:]
    hr, hi = h_ref[0], h_ref[1]
    y = jnp.concatenate([xr * hr - xi * hi, xr * hi + xi * hr], axis=0).astype(BF16)
    u = _dot(f2_ref[...], y)
    o_ref[0] = (g_ref[0].astype(F32) * (u + bias_ref[...] * xin.astype(F32))).astype(o_ref.dtype)


def direct_conv(f1, f2, h, x, g, bias_row):
    p, n2, d = x.shape
    td = 512
    xspec = pl.BlockSpec((1, n2, td), lambda i, j: (i, 0, j))
    return pl.pallas_call(
        _direct_conv_kernel, grid=(p, d // td),
        in_specs=[pl.BlockSpec(f1.shape, lambda i, j: (0, 0)),
                  pl.BlockSpec(f2.shape, lambda i, j: (0, 0)),
                  pl.BlockSpec((2, n2, td), lambda i, j: (0, 0, j)),
                  xspec, xspec, pl.BlockSpec((1, td), lambda i, j: (0, j))],
        out_specs=xspec,
        out_shape=jax.ShapeDtypeStruct((p, n2, d), BF16),
        compiler_params=_cparams(("arbitrary", "arbitrary")),
        name="direct_conv",
    )(f1, f2, h, x, g, bias_row)


def _cplx_real_form(c):
    return np.block([[c.real, -c.imag], [c.imag, c.real]])


def _dft_tables(n):
    nn = 2 * n
    r = FFT_SLAB
    if nn <= 4 * r:
        k = np.arange(nn)[:, None]
        m = np.arange(nn)[None, :]
        fc = np.exp(-2j * np.pi * ((k * m) % nn) / nn)
        f1 = _cplx_real_form(fc[:, :n])
        f2 = _cplx_real_form(np.conj(fc.T)[:n, :] / nn)
        fh = np.concatenate([fc.real, fc.imag], axis=0)
        return dict(f1=jnp.asarray(f1, BF16), f2=jnp.asarray(f2, BF16), fh=jnp.asarray(fh, BF16))
    a = nn // r
    ka = np.arange(a)[:, None]
    aa = np.arange(a)[None, :]
    fa = np.exp(-2j * np.pi * ((ka * aa) % a) / a)
    fa_fwd = _cplx_real_form(fa[:, :a // 2])
    fa_inv = _cplx_real_form(np.conj(fa.T)[:a // 2, :] / nn)
    fa_h = np.concatenate([fa.real, fa.imag], axis=0)
    kb = np.arange(r)[None, :, None]
    bb = np.arange(r)[None, None, :]
    kk = np.arange(a)[:, None, None]
    m1c = np.exp(-2j * np.pi * ((bb * (kk + a * kb)) % nn) / nn)
    m1 = np.stack([_cplx_real_form(m1c[i]) for i in range(a)])
    m2 = np.stack([_cplx_real_form(np.conj(m1c[i].T)) for i in range(a)])
    eye = np.eye(FFT_SUB)
    kron = lambda f: jnp.asarray(np.kron(f, eye), BF16)
    return dict(fa_fwd=kron(fa_fwd), fa_inv=kron(fa_inv), fa_h=kron(fa_h),
                m1=jnp.asarray(m1, BF16), m2=jnp.asarray(m2, BF16))


def hyena_long_convs(x1, x2, v, hfull, f_bias):
    b, n, d = v.shape
    nn = 2 * n
    r = FFT_SLAB
    tabs = _dft_tables(n)
    p = b // 2
    bias = f_bias.astype(F32)
    if 'f1' in tabs:
        h = bmm_left(tabs['fh'], hfull, F32).reshape(HYENA_ORDER, 2, nn, d)
        pair = lambda t: t.reshape(p, nn, d)
        z = direct_conv(tabs['f1'], tabs['f2'], h[0], pair(v), pair(x1), bias[0].reshape(1, d))
        y = direct_conv(tabs['f1'], tabs['f2'], h[1], z, pair(x2), bias[1].reshape(1, d))
        return y.reshape(b, n, d)
    a = nn // r
    th = stride_dft(tabs['fa_h'], hfull.reshape(HYENA_ORDER, a, r, d), BF16)
    h = [slab_spectrum(tabs['m1'], th[o].reshape(2, a, r, d)) for o in range(HYENA_ORDER)]
    pair = lambda t: t.reshape(p, a, r, d)

    def conv(sig, gate, o):
        t = stride_dft(tabs['fa_fwd'], pair(sig), BF16)
        u = slab_conv(tabs['m1'], tabs['m2'], h[o], t.reshape(p, 2, a, r, d))
        return stride_dft(tabs['fa_inv'], u.reshape(p, 2 * a, r, d), BF16,
                          gate=(pair(gate), pair(sig), bias[o].reshape(1, d)))

    z = conv(v, x1, 0)
    y = conv(z.reshape(b, n, d), x2, 1)
    return y.reshape(b, n, d)


def hyena_mixer(u_lat, u_ctx, w_in, conv_w, conv_b, f_w1, f_b1, f_w2, f_b2, f_w3, f_b3, f_w4,
                f_freq, f_bias, ctx_out):
    d = D_MODEL
    w = w_in.astype(BF16)

    def run(u):
        n = u.shape[1]
        hfull = hyena_filters(n, f_w1, f_b1, f_w2, f_b2, f_w3, f_b3, f_w4, f_freq)
        xin = matmul(u, w, BF16, 512)
        x1, x2, v = [dwconv(xin, conv_w[:, i * d:(i + 1) * d], conv_b[i * d:(i + 1) * d], i * d, d, act=False)
                     for i in range(3)]
        return hyena_long_convs(x1, x2, v, hfull, f_bias)

    return run(u_lat), (run(u_ctx) if ctx_out else None)


def kernel(x, c, ctx, c_ctx, ada_w, ada_b, ln_g, ln_b, mlp_w1, mlp_w2, ssd_w_in, ssd_conv_w, ssd_conv_b, ssd_dt_bias, ssd_a_log, ssd_d, ssd_norm_g, ssd_w_out, attn_w_qkv, attn_sink, attn_w_o, hy_w_in, hy_conv_w, hy_conv_b, hy_f_w1, hy_f_b1, hy_f_w2, hy_f_b2, hy_f_w3, hy_f_b3, hy_f_w4, hy_f_freq, hy_f_bias, hy_w_out):
    bsz, _, d = x.shape
    depth = ada_w.shape[0]
    assert bsz + 1 <= COND_ROWS and bsz % 2 == 0
    cond = jnp.zeros((COND_ROWS, d), F32).at[:bsz].set(c.astype(F32)).at[bsz].set(c_ctx.astype(F32))
    mods = ada_mods(cond, ada_w.astype(F32), ada_b.astype(F32))

    def mod_l(i, k):
        return mods[i, :bsz, k * d:(k + 1) * d][:, None, :]

    def mod_c(i, k):
        return jnp.broadcast_to(mods[i, bsz, k * d:(k + 1) * d][None, None, :], (bsz, 1, d))

    zero = jnp.zeros((bsz, 1, d), F32)
    xl, xc = x.astype(F32), ctx.astype(F32)
    ul = modulate(xl, mod_l(0, 1), mod_l(0, 0))
    uc = modulate(xc, mod_c(0, 1), mod_c(0, 0))
    for i in range(depth):
        last = i == depth - 1
        kind = MIXER_OF_LAYER[i]
        j = MIXER_OF_LAYER[:i].count(kind)
        if kind == 0:
            yl, yc = ssd_mixer(ul, uc, ssd_w_in[j], ssd_conv_w[j].astype(F32), ssd_conv_b[j].astype(F32),
                               ssd_dt_bias[j], ssd_a_log[j], ssd_d[j], ssd_norm_g[j], not last)
            w_out = ssd_w_out[j]
        elif kind == 1:
            yl, yc = attn_mixer(ul, uc, attn_w_qkv[j], attn_sink[j], not last)
            w_out = attn_w_o[j]
        else:
            yl, yc = hyena_mixer(ul, uc, hy_w_in[j], hy_conv_w[j].astype(F32), hy_conv_b[j].astype(F32),
                                 hy_f_w1[j], hy_f_b1[j], hy_f_w2[j], hy_f_b2[j], hy_f_w3[j], hy_f_b3[j],
                                 hy_f_w4[j], hy_f_freq[j], hy_f_bias[j], not last)
            w_out = hy_w_out[j]
        w_out = w_out.astype(BF16)
        w1 = mlp_w1[i].astype(BF16)
        w2 = mlp_w2[i].astype(BF16)
        nsc_l, nsh_l = (zero, zero) if last else (mod_l(i + 1, 1), mod_l(i + 1, 0))
        xl, ul = matmul_res_ln(yl, w_out, xl, mod_l(i, 2), ln_g[i, 0], ln_b[i, 0], mod_l(i, 4), mod_l(i, 3))
        xl, ul = mlp_res_ln(ul, w1, w2, xl, mod_l(i, 5), ln_g[i, 1], ln_b[i, 1], nsc_l, nsh_l)
        if not last:
            xc, uc = matmul_res_ln(yc, w_out, xc, mod_c(i, 2), ln_g[i, 0], ln_b[i, 0], mod_c(i, 4), mod_c(i, 3))
            xc, uc = mlp_res_ln(uc, w1, w2, xc, mod_c(i, 5), ln_g[i, 1], ln_b[i, 1],
                                mod_c(i + 1, 1), mod_c(i + 1, 0))
    return xl.astype(x.dtype)
```

```python
import functools
import math

import numpy as np
import jax
import jax.numpy as jnp
from jax import lax
from jax.experimental import pallas as pl
from jax.experimental.pallas import tpu as pltpu

F32 = jnp.float32
BF16 = jnp.bfloat16
HIGHEST = lax.Precision.HIGHEST

D_MODEL = 2048
DEPTH = 4
GRID_W = 64
N_MIXERS = 3
MIXER_OF_LAYER = tuple(i % N_MIXERS for i in range(DEPTH))
ALPHA = (2.0 * DEPTH) ** 0.25
LN_EPS = 1e-5
RMS_EPS = 1e-5
N_MOD = 6
MLP_HIDDEN = 4 * D_MODEL

D_INNER = 2 * D_MODEL
SSM_HEAD_DIM = 64
SSM_HEADS = D_INNER // SSM_HEAD_DIM
SSM_GROUPS = 8
SSM_HPG = SSM_HEADS // SSM_GROUPS
SSM_STATE = 128
SSM_CONV_W = 5
SSM_CHUNK = 128
SSM_XBC = D_INNER + 2 * SSM_GROUPS * SSM_STATE
SSM_GW = D_INNER // SSM_GROUPS
SSM_SPLIT_ROWS = 6 * SSM_HPG
SSM_GPS = 8
LOG2E = math.log2(math.e)

ATTN_HEAD_DIM = 64
ATTN_Q_HEADS = D_MODEL // ATTN_HEAD_DIM
ATTN_KV_HEADS = 4
ATTN_GROUP = ATTN_Q_HEADS // ATTN_KV_HEADS
ATTN_Q_W = ATTN_Q_HEADS * ATTN_HEAD_DIM
ATTN_KV_W = ATTN_KV_HEADS * ATTN_HEAD_DIM
WINDOW = 128
ATTN_BLOCK = 128
ROPE_BASE = 10000.0
ATTN_MASKED = -1e30

HYENA_ORDER = 2
HYENA_SHORT_W = 3
HYENA_EMB = 33
HYENA_FILTER_W = 64
HYENA_DECAY_FAST = 0.3
HYENA_DECAY_SLOW = 1.5
HYENA_DECAY_TARGET = 1e-2

LANE = 128
FFT_SLAB = 128
FFT_SUB = 16
COND_ROWS = 16
VMEM_LIMIT_MB = 56


def _cparams(sem, vmem_mb=VMEM_LIMIT_MB):
    return pltpu.CompilerParams(dimension_semantics=sem,
                                vmem_limit_bytes=vmem_mb * 1024 * 1024)


def _dot(a, b, precision=None):
    return jnp.dot(a, b, preferred_element_type=F32, precision=precision)


def _dot_nt(a, b):
    return lax.dot_general(a, b, (((1,), (1,)), ((), ())), preferred_element_type=F32)


def _dot_tn(a, b):
    return lax.dot_general(a, b, (((0,), (0,)), ((), ())), preferred_element_type=F32)


def _silu(x):
    return x * jax.nn.sigmoid(x)


def _split_parts(v, parts):
    out = []
    for _ in range(parts):
        piece = v.astype(BF16).astype(F32)
        out.append(piece)
        v = v - piece
    return out


def _ada_kernel(c_ref, w_ref, b_ref, o_ref):
    s = _silu(c_ref[...])
    o_ref[0] = _dot(s, w_ref[0], HIGHEST) + b_ref[0]


def ada_mods(cond, ada_w, ada_b):
    depth, d, n6 = ada_w.shape
    tn = 512
    return pl.pallas_call(
        _ada_kernel,
        grid=(depth, n6 // tn),
        in_specs=[pl.BlockSpec((COND_ROWS, d), lambda l, j: (0, 0)),
                  pl.BlockSpec((1, d, tn), lambda l, j: (l, 0, j)),
                  pl.BlockSpec((1, 1, tn), lambda l, j: (l, 0, j))],
        out_specs=pl.BlockSpec((1, COND_ROWS, tn), lambda l, j: (l, 0, j)),
        out_shape=jax.ShapeDtypeStruct((depth, COND_ROWS, n6), F32),
        compiler_params=_cparams(("arbitrary", "arbitrary")),
        name="ada_mods",
    )(cond, ada_w, ada_b.reshape(depth, 1, n6))


def _modulate_kernel(x_ref, sc_ref, sh_ref, o_ref):
    o_ref[0] = (x_ref[0] * (1.0 + sc_ref[0]) + sh_ref[0]).astype(o_ref.dtype)


def modulate(x, sc, sh):
    b, n, d = x.shape
    tm = min(n, 1024)
    row = pl.BlockSpec((1, tm, d), lambda i, j: (i, j, 0))
    vec = pl.BlockSpec((1, 1, d), lambda i, j: (i, 0, 0))
    return pl.pallas_call(
        _modulate_kernel, grid=(b, n // tm),
        in_specs=[row, vec, vec], out_specs=row,
        out_shape=jax.ShapeDtypeStruct((b, n, d), BF16),
        compiler_params=_cparams(("arbitrary", "arbitrary")),
        name="modulate",
    )(x, sc, sh)


def _mm_kernel(a_ref, w_ref, o_ref):
    o_ref[0] = _dot(a_ref[0], w_ref[...]).astype(o_ref.dtype)


def matmul(a, w, out_dtype, tn):
    b, n, k = a.shape
    nn = w.shape[1]
    tm = min(n, 1024)
    tn = min(tn, nn)
    return pl.pallas_call(
        _mm_kernel, grid=(b, n // tm, nn // tn),
        in_specs=[pl.BlockSpec((1, tm, k), lambda i, j, l: (i, j, 0)),
                  pl.BlockSpec((k, tn), lambda i, j, l: (0, l))],
        out_specs=pl.BlockSpec((1, tm, tn), lambda i, j, l: (i, j, l)),
        out_shape=jax.ShapeDtypeStruct((b, n, nn), out_dtype),
        compiler_params=_cparams(("arbitrary", "arbitrary", "arbitrary")),
        name="matmul",
    )(a, w)


def _res_ln(x, y, gate, g, b):
    h = ALPHA * x + gate * y
    mu = jnp.mean(h, axis=-1, keepdims=True)
    hc = h - mu
    var = jnp.mean(hc * hc, axis=-1, keepdims=True)
    return hc * lax.rsqrt(var + LN_EPS) * g + b


def _mm_res_ln_kernel(a_ref, w_ref, x_ref, gate_ref, g_ref, b_ref, sc_ref, sh_ref, xo_ref, uo_ref):
    xn = _res_ln(x_ref[0], _dot(a_ref[0], w_ref[...]), gate_ref[0], g_ref[...], b_ref[...])
    xo_ref[0] = xn
    uo_ref[0] = (xn * (1.0 + sc_ref[0]) + sh_ref[0]).astype(uo_ref.dtype)


def matmul_res_ln(a, w, x, gate, ln_g, ln_b, sc_next, sh_next):
    b, n, kk = a.shape
    d = w.shape[1]
    tm = min(n, 512)
    row = pl.BlockSpec((1, tm, d), lambda i, j: (i, j, 0))
    vec = pl.BlockSpec((1, 1, d), lambda i, j: (i, 0, 0))
    par = pl.BlockSpec((1, d), lambda i, j: (0, 0))
    return pl.pallas_call(
        _mm_res_ln_kernel, grid=(b, n // tm),
        in_specs=[pl.BlockSpec((1, tm, kk), lambda i, j: (i, j, 0)),
                  pl.BlockSpec((kk, d), lambda i, j: (0, 0), pipeline_mode=pl.Buffered(1)),
                  row, vec, par, par, vec, vec],
        out_specs=[row, row],
        out_shape=[jax.ShapeDtypeStruct((b, n, d), F32), jax.ShapeDtypeStruct((b, n, d), BF16)],
        compiler_params=_cparams(("arbitrary", "arbitrary")),
        name="matmul_res_ln",
    )(a, w, x, gate, ln_g.reshape(1, d), ln_b.reshape(1, d), sc_next, sh_next)


def _mlp_kernel(u_ref, w1_ref, w2_ref, x_ref, gate_ref, g_ref, b_ref, sc_ref, sh_ref,
                xo_ref, uo_ref, acc_ref):
    k = pl.program_id(2)

    @pl.when(k == 0)
    def _():
        acc_ref[...] = jnp.zeros_like(acc_ref)

    h = jnp.maximum(_dot(u_ref[0], w1_ref[...]), 0.0)
    acc_ref[...] += _dot((h * h).astype(BF16), w2_ref[...])

    @pl.when(k == pl.num_programs(2) - 1)
    def _():
        xn = _res_ln(x_ref[0], acc_ref[...], gate_ref[0], g_ref[...], b_ref[...])
        xo_ref[0] = xn
        uo_ref[0] = (xn * (1.0 + sc_ref[0]) + sh_ref[0]).astype(uo_ref.dtype)


def mlp_res_ln(u, w1, w2, x, gate, ln_g, ln_b, sc_next, sh_next):
    b, n, d = u.shape
    hid = w1.shape[1]
    tm = min(n, 512)
    th = 1024
    row = pl.BlockSpec((1, tm, d), lambda i, j, k: (i, j, 0))
    vec = pl.BlockSpec((1, 1, d), lambda i, j, k: (i, 0, 0))
    par = pl.BlockSpec((1, d), lambda i, j, k: (0, 0))
    return pl.pallas_call(
        _mlp_kernel, grid=(b, n // tm, hid // th),
        in_specs=[row,
                  pl.BlockSpec((d, th), lambda i, j, k: (0, k)),
                  pl.BlockSpec((th, d), lambda i, j, k: (k, 0)),
                  row, vec, par, par, vec, vec],
        out_specs=[row, row],
        out_shape=[jax.ShapeDtypeStruct((b, n, d), F32), jax.ShapeDtypeStruct((b, n, d), BF16)],
        scratch_shapes=[pltpu.VMEM((tm, d), F32)],
        compiler_params=_cparams(("arbitrary", "arbitrary", "arbitrary")),
        name="mlp_res_ln",
    )(u, w1, w2, x, gate, ln_g.reshape(1, d), ln_b.reshape(1, d), sc_next, sh_next)


def _dwconv_kernel(x_ref, w_ref, b_ref, o_ref, *, width, act):
    x = x_ref[0].astype(F32)
    n = x.shape[0]
    half = width // 2
    rows = lax.broadcasted_iota(jnp.int32, x.shape, 0)
    acc = x * w_ref[half:half + 1, :] + b_ref[...]
    for k in range(width):
        off = k - half
        if off == 0:
            continue
        shifted = pltpu.roll(x, (-off) % n, axis=0)
        valid = (rows + off >= 0) & (rows + off < n)
        acc = acc + jnp.where(valid, shifted, 0.0) * w_ref[k:k + 1, :]
    if act:
        acc = _silu(acc)
    o_ref[0] = acc.astype(o_ref.dtype)


def dwconv(x, w, bias, col0, ncols, act, out_dtype=BF16):
    b, n, _ = x.shape
    width = w.shape[0]
    tc = 256 if n > 1024 else 512
    tc = min(tc, ncols)
    cb0 = col0 // tc
    return pl.pallas_call(
        functools.partial(_dwconv_kernel, width=width, act=act),
        grid=(b, ncols // tc),
        in_specs=[pl.BlockSpec((1, n, tc), lambda i, j: (i, 0, cb0 + j)),
                  pl.BlockSpec((width, tc), lambda i, j: (0, j)),
                  pl.BlockSpec((1, tc), lambda i, j: (0, j))],
        out_specs=pl.BlockSpec((1, n, tc), lambda i, j: (i, 0, j)),
        out_shape=jax.ShapeDtypeStruct((b, n, ncols), out_dtype),
        compiler_params=_cparams(("arbitrary", "arbitrary")),
        name="dwconv",
    )(x, w, bias.reshape(1, ncols))


def _dtprep_kernel(raw_ref, bias_ref, a_ref, at_ref, r0_ref, ft_ref):
    x = raw_ref[0] + bias_ref[...]
    dt = jnp.maximum(x, 0.0) + jnp.log1p(jnp.exp(-jnp.abs(x)))
    adt = dt * a_ref[...]
    q = x.shape[0]
    r = lax.broadcasted_iota(jnp.int32, (q, q), 0)
    c = lax.broadcasted_iota(jnp.int32, (q, q), 1)
    fwd = _dot((c <= r).astype(F32), adt, HIGHEST)
    bwd = _dot((c >= r).astype(F32), adt, HIGHEST)
    lane = lax.broadcasted_iota(jnp.int32, x.shape, 1)
    cum = jnp.where(lane < SSM_HEADS, fwd, bwd)
    tot = jnp.where(lane[0:1] < SSM_HEADS, cum[q - 1:q], cum[0:1])
    c3 = _split_parts((cum * LOG2E).T, 3)
    fparts = [s for v in (dt, jnp.exp(cum), jnp.exp(tot - cum) * dt) for s in _split_parts(v.T, 2)]
    ones = jnp.ones((3 * SSM_HPG, q), F32)
    for d in range(2):
        for g in range(SSM_GROUPS):
            rows = slice(d * SSM_HEADS + g * SSM_HPG, d * SSM_HEADS + (g + 1) * SSM_HPG)
            at_ref[0, 0, d, g] = jnp.concatenate([s[rows] for s in c3] + [ones], axis=0).astype(BF16)
            r0_ref[0, 0, d, g] = jnp.concatenate([ones] + [-s[rows] for s in c3], axis=0).astype(BF16)
            ft_ref[0, 0, d, g] = jnp.concatenate([s[rows] for s in fparts], axis=0).astype(BF16)


def ssd_dtprep(dt_raw, dt_bias, a_log):
    b, n, w = dt_raw.shape
    q = SSM_CHUNK
    nc = n // q
    a = -jnp.exp(a_log.astype(F32)).reshape(1, w)
    par = pl.BlockSpec((1, w), lambda i, j: (0, 0))
    oshape = (b, nc, 2, SSM_GROUPS, SSM_SPLIT_ROWS, q)
    ospec = pl.BlockSpec((1, 1) + oshape[2:], lambda i, j: (i, j, 0, 0, 0, 0))
    return pl.pallas_call(
        _dtprep_kernel, grid=(b, nc),
        in_specs=[pl.BlockSpec((1, q, w), lambda i, j: (i, j, 0)), par, par],
        out_specs=[ospec] * 3,
        out_shape=[jax.ShapeDtypeStruct(oshape, BF16)] * 3,
        compiler_params=_cparams(("arbitrary", "arbitrary")),
        name="ssd_dtprep",
    )(dt_raw, dt_bias.reshape(1, w).astype(F32), a)


def _ssd_scan_kernel(*refs, reverse, finish):
    if finish:
        (x_ref, b_ref, c_ref, at_ref, r0_ref, ft_ref, bmask_ref, eexp_ref, h0_ref,
         yf_ref, z_ref, dsk_ref, ng_ref, y_ref, hout_ref, h_scr) = refs
    else:
        (x_ref, b_ref, c_ref, at_ref, r0_ref, ft_ref, bmask_ref, eexp_ref, h0_ref,
         y_ref, hout_ref, h_scr) = refs
    step = pl.program_id(2)

    @pl.when(step == 0)
    def _():
        h_scr[...] = h0_ref[0]

    q = SSM_CHUNK
    p = SSM_HEAD_DIM
    gw = SSM_GW
    ns = SSM_STATE
    ri = lax.broadcasted_iota(jnp.int32, (q, q), 0)
    ci = lax.broadcasted_iota(jnp.int32, (q, q), 1)
    keep = (ci >= ri) if reverse else (ci <= ri)
    left = lax.broadcasted_iota(jnp.int32, (q, 2 * p), 1) < p

    for gi in range(SSM_GPS):
        gcols = slice(gi * gw, (gi + 1) * gw)
        x = x_ref[0, :, gcols].astype(F32)
        bm = b_ref[0, :, gi * ns:(gi + 1) * ns]
        cm = c_ref[0, :, gi * ns:(gi + 1) * ns]
        cb = jnp.where(keep, _dot_nt(cm, bm), 0.0)
        rhs = jnp.concatenate([r0_ref[0, 0, 0, gi]] * SSM_HPG, axis=1) * bmask_ref[...]
        diff = _dot_tn(at_ref[0, 0, 0, gi], rhs)
        fac = _dot_tn(ft_ref[0, 0, 0, gi], eexp_ref[...])
        xdt = x * fac[:, :gw]
        xw = (x * fac[:, 2 * gw:]).astype(BF16)
        ec = fac[:, gw:2 * gw]
        etot = ec[0:1] if reverse else ec[q - 1:q]
        ht = h_scr[gi]
        yoff = _dot(cm, ht.astype(BF16)) * ec

        ys = []
        for pr in range(SSM_HPG // 2):
            cols = slice(pr * 2 * p, (pr + 1) * 2 * p)
            xp = xdt[:, cols]
            yp = yoff[:, cols]
            for hd, xm in ((2 * pr, jnp.where(left, xp, 0.0)), (2 * pr + 1, jnp.where(left, 0.0, xp))):
                dec = jnp.exp2(jnp.minimum(diff[:, hd * q:(hd + 1) * q], 0.0))
                yp = yp + _dot((cb * dec).astype(BF16), xm.astype(BF16))
            ys.append(yp)
        y = jnp.concatenate(ys, axis=1)
        h_scr[gi] = ht * etot + _dot_tn(bm, xw)

        if finish:
            yt = y + yf_ref[0, :, gcols] + dsk_ref[:, gcols] * x
            yt = yt * _silu(z_ref[0, :, gcols].astype(F32))
            ms = jnp.mean(yt * yt, axis=-1, keepdims=True)
            y_ref[0, :, gcols] = (yt * lax.rsqrt(ms + RMS_EPS) * ng_ref[:, gcols]).astype(y_ref.dtype)
        else:
            y_ref[0, :, gcols] = y

    @pl.when(step == pl.num_programs(2) - 1)
    def _():
        hout_ref[0] = h_scr[...]


def _ssd_tables():
    q, hg, p = SSM_CHUNK, SSM_HPG, SSM_HEAD_DIM
    r = np.arange(SSM_SPLIT_ROWS)[:, None]
    bmask = (r % hg == np.arange(hg * q)[None, :] // q)
    c = np.arange(3 * SSM_GW)[None, :]
    eexp = (r // (2 * hg) == c // SSM_GW) & (r % hg == (c % SSM_GW) // p)
    return jnp.asarray(bmask, BF16), jnp.asarray(eexp, BF16)


def ssd_scan(xbc, at, r0, ft, h0, *, reverse, fin=None):
    b, n, _ = xbc.shape
    q, g, gw, ns = SSM_CHUNK, SSM_GROUPS, SSM_GW, SSM_STATE
    sr = SSM_SPLIT_ROWS
    nc = n // q
    bmask, eexp = _ssd_tables()

    def ch(k):
        return nc - 1 - k if reverse else k

    gps = SSM_GPS
    bcol = D_INNER // (gps * ns)
    direction = 1 if reverse else 0
    opspec = pl.BlockSpec((1, 1, 1, gps, sr, q), lambda i, j, k: (i, ch(k), direction, j, 0, 0))
    in_specs = [
        pl.BlockSpec((1, q, gps * gw), lambda i, j, k: (i, ch(k), j)),
        pl.BlockSpec((1, q, gps * ns), lambda i, j, k: (i, ch(k), bcol + j)),
        pl.BlockSpec((1, q, gps * ns), lambda i, j, k: (i, ch(k), bcol + g // gps + j)),
        opspec, opspec, opspec,
        pl.BlockSpec(bmask.shape, lambda i, j, k: (0, 0)),
        pl.BlockSpec(eexp.shape, lambda i, j, k: (0, 0)),
        pl.BlockSpec((1, gps, ns, gw), lambda i, j, k: (i, j, 0, 0)),
    ]
    args = [xbc, xbc, xbc, at, r0, ft, bmask, eexp, h0]
    yspec = pl.BlockSpec((1, q, gps * gw), lambda i, j, k: (i, ch(k), j))
    if fin is not None:
        y_other, z, dsk, ng = fin
        in_specs += [yspec, yspec,
                     pl.BlockSpec((1, gps * gw), lambda i, j, k: (0, j)),
                     pl.BlockSpec((1, gps * gw), lambda i, j, k: (0, j))]
        args += [y_other, z, dsk, ng]
    return pl.pallas_call(
        functools.partial(_ssd_scan_kernel, reverse=reverse, finish=fin is not None),
        grid=(b, g // gps, nc),
        in_specs=in_specs,
        out_specs=[yspec, pl.BlockSpec((1, gps, ns, gw), lambda i, j, k: (i, j, 0, 0))],
        out_shape=[jax.ShapeDtypeStruct((b, n, D_INNER), BF16 if fin is not None else F32),
                   jax.ShapeDtypeStruct((b, g, ns, gw), F32)],
        scratch_shapes=[pltpu.VMEM((gps, ns, gw), F32)],
        compiler_params=_cparams(("arbitrary", "arbitrary", "arbitrary")),
        name="ssd_scan_bwd" if reverse else "ssd_scan_fwd",
    )(*args)


def ssd_mixer(u_lat, u_ctx, w_in, conv_w, conv_b, dt_bias, a_log, d_skip, norm_g, ctx_out):
    g, hg = SSM_GROUPS, SSM_HPG
    w_z = w_in[:, :D_INNER].astype(BF16)
    w_xbc = w_in[:, D_INNER:D_INNER + SSM_XBC].astype(BF16)
    w_dt = w_in[:, D_INNER + SSM_XBC:].astype(BF16)
    dsk = jnp.repeat(d_skip.astype(F32), SSM_HEAD_DIM).reshape(1, D_INNER)
    ng = norm_g.astype(F32).reshape(1, D_INNER)

    def project(u):
        b, n, _ = u.shape
        z = matmul(u, w_z, BF16, 512)
        xbc = dwconv(matmul(u, w_xbc, BF16, 512), conv_w, conv_b, 0, SSM_XBC, act=True)
        at, r0, ft = ssd_dtprep(matmul(u, w_dt, F32, 2 * SSM_HEADS), dt_bias, a_log)
        return z, xbc, at, r0, ft

    def bidir(proj, h_f, h_b, want_y):
        z, xbc, at, r0, ft = proj
        y_f, s_f = ssd_scan(xbc, at, r0, ft, h_f, reverse=False)
        fin = (y_f, z, dsk, ng) if want_y else None
        y, s_b = ssd_scan(xbc, at, r0, ft, h_b, reverse=True, fin=fin)
        return y, s_f, s_b

    h0 = jnp.zeros((u_lat.shape[0], g, SSM_STATE, SSM_GW), F32)
    yc, hc_f, hc_b = bidir(project(u_ctx), h0, h0, ctx_out)
    yl, _, _ = bidir(project(u_lat), hc_f, hc_b, True)
    return yl, (yc if ctx_out else None)


def _rope_kernel(x_ref, cos_ref, sin_ref, o_ref):
    cos = cos_ref[...]
    sin = sin_ref[...]
    half = ATTN_HEAD_DIM // 2
    first = (lax.broadcasted_iota(jnp.int32, cos.shape, 1) % ATTN_HEAD_DIM) < half
    for t in range(x_ref.shape[2] // LANE):
        cols = slice(t * LANE, (t + 1) * LANE)
        x = x_ref[0, :, cols].astype(F32)
        partner = jnp.where(first, pltpu.roll(x, LANE - half, axis=1), pltpu.roll(x, half, axis=1))
        o_ref[0, :, cols] = (x * cos + partner * sin).astype(o_ref.dtype)


def rope(qkv, cos_t, sin_t):
    b, n, _ = qkv.shape
    w = ATTN_Q_W + ATTN_KV_W
    tm = 512
    return pl.pallas_call(
        _rope_kernel, grid=(b, n // tm),
        in_specs=[pl.BlockSpec((1, tm, w), lambda i, j: (i, j, 0)),
                  pl.BlockSpec((tm, LANE), lambda i, j: (j, 0)),
                  pl.BlockSpec((tm, LANE), lambda i, j: (j, 0))],
        out_specs=pl.BlockSpec((1, tm, w), lambda i, j: (i, j, 0)),
        out_shape=jax.ShapeDtypeStruct((b, n, w), BF16),
        compiler_params=_cparams(("arbitrary", "arbitrary")),
        name="rope",
    )(qkv, cos_t, sin_t)


def _attn_kernel(*refs, local, seq_len):
    if local:
        (q_ref, k0_ref, k1_ref, k2_ref, v0_ref, v1_ref, v2_ref, kc_ref, vc_ref, sink_ref, band_ref,
         o_ref, qx_ref, kx_ref, vx_ref) = refs
        kv_blocks = ((k0_ref, v0_ref), (k1_ref, v1_ref), (k2_ref, v2_ref))
    else:
        q_ref, kc_ref, vc_ref, sink_ref, o_ref, qx_ref, kx_ref, vx_ref = refs
        kv_blocks = ()
    blk, dh, grp = ATTN_BLOCK, ATTN_HEAD_DIM, ATTN_GROUP
    rows = grp * blk
    nloc = len(kv_blocks) * blk
    pad = LANE - dh

    @pl.when((pl.program_id(0) == 0) & (pl.program_id(1) == 0))
    def _():
        r = lax.broadcasted_iota(jnp.int32, (rows, LANE), 0) % blk
        c = lax.broadcasted_iota(jnp.int32, (rows, LANE), 1)
        cpad = lax.broadcasted_iota(jnp.int32, (rows, pad), 1)
        for kv in range(ATTN_KV_HEADS):
            qx_ref[kv, :, 0:LANE] = jnp.where(r == c, 1.0, 0.0).astype(BF16)
            qx_ref[kv, :, LANE + dh:] = jnp.where(cpad == 0, 1.0, 0.0).astype(BF16)
            kx_ref[kv] = jnp.zeros(kx_ref.shape[1:], BF16)
            if local:
                kx_ref[kv, 0:nloc, 0:LANE] = band_ref[...]
            vx_ref[kv, :, dh:] = jnp.ones((vx_ref.shape[1], pad), BF16)

    if local:
        sj = lax.broadcasted_iota(jnp.int32, (nloc, pad), 0)
        lane = lax.broadcasted_iota(jnp.int32, (nloc, pad), 1)
        kpos = (pl.program_id(1) - 1) * blk + sj
        outside = (kpos < 0) | (kpos >= seq_len)
        edge = jnp.where((lane == 0) & outside, ATTN_MASKED, 0.0).astype(BF16)

    for kv in range(ATTN_KV_HEADS):
        hc = slice(kv * dh, (kv + 1) * dh)
        if local:
            kx_ref[kv, 0:nloc, LANE + dh:] = edge
        for g in range(grp):
            c0 = (kv * grp + g) * dh
            qx_ref[kv, g * blk:(g + 1) * blk, LANE:LANE + dh] = q_ref[0, :, c0:c0 + dh]
        for t, (k_ref, v_ref) in enumerate(kv_blocks):
            kx_ref[kv, t * blk:(t + 1) * blk, LANE:LANE + dh] = k_ref[0, :, hc]
            vx_ref[kv, t * blk:(t + 1) * blk, 0:dh] = v_ref[0, :, hc]
        kx_ref[kv, nloc:, LANE:LANE + dh] = kc_ref[0, :, hc]
        vx_ref[kv, nloc:, 0:dh] = vc_ref[0, :, hc]
        sink = sink_ref[kv]
        s = _dot_nt(qx_ref[kv], kx_ref[kv])
        m = jnp.maximum(jnp.max(s, axis=-1, keepdims=True), sink)
        p = jnp.exp(s - m).astype(BF16)
        oe = _dot(p, vx_ref[kv])
        den = pltpu.roll(oe, dh, axis=1) + jnp.exp(sink - m)
        o = (oe / den).astype(o_ref.dtype)
        for g in range(grp):
            c0 = (kv * grp + g) * dh
            o_ref[0, :, c0:c0 + dh] = o[g * blk:(g + 1) * blk, 0:dh]


def attention(qk, qkv, qkv_ctx, sink_col, *, local):
    b, n, _ = qk.shape
    nctx = qkv_ctx.shape[1]
    blk = ATTN_BLOCK
    nblk = n // blk
    kcb = ATTN_Q_W // ATTN_KV_W
    vcb = kcb + 1
    qspec = pl.BlockSpec((1, blk, ATTN_Q_W), lambda i, j: (i, j, 0))
    ctx_k = pl.BlockSpec((1, nctx, ATTN_KV_W), lambda i, j: (i, 0, kcb))
    ctx_v = pl.BlockSpec((1, nctx, ATTN_KV_W), lambda i, j: (i, 0, vcb))
    sspec = pl.BlockSpec(sink_col.shape, lambda i, j: (0, 0, 0))
    if local:
        def win(cb, off):
            return pl.BlockSpec((1, blk, ATTN_KV_W),
                                lambda i, j: (i, jnp.clip(j + off, 0, nblk - 1), cb))
        sidx = np.arange(3 * blk)[:, None]
        ridx = np.arange(blk)[None, :]
        band = jnp.asarray(np.where(np.abs(sidx - blk - ridx) <= WINDOW, 0.0, ATTN_MASKED), BF16)
        in_specs = [qspec, win(kcb, -1), win(kcb, 0), win(kcb, 1),
                    win(vcb, -1), win(vcb, 0), win(vcb, 1), ctx_k, ctx_v, sspec,
                    pl.BlockSpec(band.shape, lambda i, j: (0, 0))]
        args = [qk, qk, qk, qk, qkv, qkv, qkv, qkv_ctx, qkv_ctx, sink_col, band]
        nkeys = 3 * blk + nctx
    else:
        in_specs = [qspec, ctx_k, ctx_v, sspec]
        args = [qk, qkv_ctx, qkv_ctx, sink_col]
        nkeys = nctx
    return pl.pallas_call(
        functools.partial(_attn_kernel, local=local, seq_len=n),
        grid=(b, nblk),
        in_specs=in_specs,
        out_specs=qspec,
        out_shape=jax.ShapeDtypeStruct((b, n, ATTN_Q_W), BF16),
        scratch_shapes=[pltpu.VMEM((ATTN_KV_HEADS, ATTN_GROUP * blk, 2 * LANE), BF16),
                        pltpu.VMEM((ATTN_KV_HEADS, nkeys, 2 * LANE), BF16),
                        pltpu.VMEM((ATTN_KV_HEADS, nkeys, LANE), BF16)],
        compiler_params=_cparams(("arbitrary", "arbitrary")),
        name="attention_local" if local else "attention_ctx",
    )(*args)


def _rope_tables(n):
    rows = n // GRID_W
    row_id = np.repeat(np.arange(rows, dtype=np.float32), GRID_W)
    col_id = np.tile(np.arange(GRID_W, dtype=np.float32), rows)
    pairs = ATTN_HEAD_DIM // 4
    inv = (np.float32(ROPE_BASE) ** (-np.arange(pairs, dtype=np.float32) / np.float32(pairs))).astype(np.float32)
    ang = np.concatenate([row_id[:, None] * inv, col_id[:, None] * inv], axis=-1).astype(np.float32)
    cos, sin = np.cos(ang.astype(np.float64)), np.sin(ang.astype(np.float64))
    reps = LANE // ATTN_HEAD_DIM
    cos_t = np.tile(np.concatenate([cos, cos], axis=-1), (1, reps))
    sin_t = np.tile(np.concatenate([-sin, sin], axis=-1), (1, reps))
    return jnp.asarray(cos_t, F32), jnp.asarray(sin_t, F32)


def attn_mixer(u_lat, u_ctx, w_qkv, sink, ctx_out):
    dh = ATTN_HEAD_DIM
    perm = np.concatenate([np.arange(0, dh, 2), np.arange(1, dh, 2)])
    nqk = ATTN_Q_HEADS + ATTN_KV_HEADS
    cols = (np.arange(nqk)[:, None] * dh + perm[None, :]).reshape(-1)
    cols = np.concatenate([cols, np.arange(nqk * dh, nqk * dh + ATTN_KV_W)])
    colscale = np.where(np.arange(cols.shape[0]) < ATTN_Q_W, dh ** -0.5, 1.0).astype(np.float32)
    w = (w_qkv[:, cols] * colscale).astype(BF16)
    sink_col = jnp.repeat(sink.astype(F32).reshape(ATTN_KV_HEADS, ATTN_GROUP), ATTN_BLOCK, axis=1)
    sink_col = sink_col.reshape(ATTN_KV_HEADS, ATTN_GROUP * ATTN_BLOCK, 1)

    qkv_l = matmul(u_lat, w, BF16, 512)
    qkv_c = matmul(u_ctx, w, BF16, 512)
    cos_t, sin_t = _rope_tables(u_lat.shape[1])
    qk_l = rope(qkv_l, cos_t, sin_t)
    o_l = attention(qk_l, qkv_l, qkv_c, sink_col, local=True)
    o_c = attention(qkv_c, qkv_c, qkv_c, sink_col, local=False) if ctx_out else None
    return o_l, o_c


def _filter_kernel(z_ref, t_ref, w1_ref, b1_ref, w2_ref, b2_ref, w3_ref, b3_ref, fr_ref,
                   w4f_ref, w4b_ref, dl_ref, o_ref, h_scr, *, seq_len):
    @pl.when((pl.program_id(1) == 0) & (pl.program_id(2) == 0))
    def _():
        fr = fr_ref[...]
        h = jnp.sin(fr * (_dot(z_ref[...], w1_ref[...], HIGHEST) + b1_ref[...]))
        h = jnp.sin(fr * (_dot(h, w2_ref[...], HIGHEST) + b2_ref[...]))
        h_scr[...] = jnp.sin(fr * (_dot(h, w3_ref[...], HIGHEST) + b3_ref[...]))

    h = h_scr[...]
    tr = h.shape[0]
    r = pl.program_id(0) * tr + lax.broadcasted_iota(jnp.int32, (tr, 1), 0)
    hf = _dot(h, w4f_ref[...], HIGHEST)
    hb = _dot(h, w4b_ref[...], HIGHEST)
    val = jnp.where(r < seq_len, hf, hb) * jnp.exp(-t_ref[...] * dl_ref[...])
    o_ref[0] = jnp.where(r == seq_len, 0.0, val)


def hyena_filters(n, w1, b1, w2, b2, w3, b3, w4, freq):
    d = D_MODEL
    fw = HYENA_FILTER_W
    lag = np.concatenate([np.arange(n), [0], np.arange(n - 1, 0, -1)])
    t = np.linspace(0.0, 1.0, n, dtype=np.float32)[:, None]
    bands = (HYENA_EMB - 1) // 2
    wv = (np.float32(2.0 * math.pi) * np.arange(n, dtype=np.float32)[:, None] / np.float32(n)).astype(np.float32)
    f = np.linspace(1e-4, bands - 1, bands, dtype=np.float32)[None, :]
    fwv = (f * wv).astype(np.float32).astype(np.float64)
    z = np.concatenate([t, np.cos(fwv), -np.sin(fwv)], axis=-1).astype(np.float32)
    zpad = np.zeros((2 * n, fw), np.float32)
    zpad[:, :HYENA_EMB] = z[lag]
    tfull = t[lag]
    max_decay = math.log(HYENA_DECAY_TARGET) / HYENA_DECAY_FAST
    min_decay = math.log(HYENA_DECAY_TARGET) / HYENA_DECAY_SLOW
    deltas = np.abs(np.linspace(min_decay, max_decay, d, dtype=np.float32))[None, :]
    w1p = jnp.zeros((fw, fw), F32).at[:HYENA_EMB].set(w1.astype(F32))
    tr = min(2 * n, 512)
    td = 512
    nd = d // td
    par = lambda shape: pl.BlockSpec(shape, lambda i, o, j: (0, 0))
    return pl.pallas_call(
        functools.partial(_filter_kernel, seq_len=n),
        grid=(2 * n // tr, HYENA_ORDER, nd),
        in_specs=[pl.BlockSpec((tr, fw), lambda i, o, j: (i, 0)),
                  pl.BlockSpec((tr, 1), lambda i, o, j: (i, 0)),
                  par((fw, fw)), par((1, fw)), par((fw, fw)), par((1, fw)),
                  par((fw, fw)), par((1, fw)), par((1, fw)),
                  pl.BlockSpec((fw, td), lambda i, o, j: (0, (2 * o) * nd + j)),
                  pl.BlockSpec((fw, td), lambda i, o, j: (0, (2 * o + 1) * nd + j)),
                  pl.BlockSpec((1, td), lambda i, o, j: (0, j))],
        out_specs=pl.BlockSpec((1, tr, td), lambda i, o, j: (o, i, j)),
        out_shape=jax.ShapeDtypeStruct((HYENA_ORDER, 2 * n, d), F32),
        scratch_shapes=[pltpu.VMEM((tr, fw), F32)],
        compiler_params=_cparams(("arbitrary", "arbitrary", "arbitrary")),
        name="hyena_filters",
    )(jnp.asarray(zpad), jnp.asarray(tfull), w1p, b1.astype(F32).reshape(1, fw),
      w2.astype(F32), b2.astype(F32).reshape(1, fw), w3.astype(F32), b3.astype(F32).reshape(1, fw),
      freq.astype(F32).reshape(1, fw), w4.astype(F32), w4.astype(F32), jnp.asarray(deltas))


def _bmm_left_kernel(*refs, gated):
    if gated:
        f_ref, x_ref, g_ref, v_ref, bias_ref, o_ref = refs
    else:
        f_ref, x_ref, o_ref = refs
    y = _dot(f_ref[...], x_ref[0].astype(BF16))
    if gated:
        v = v_ref[0].astype(F32)
        y = g_ref[0].astype(F32) * (y + bias_ref[...] * v)
    o_ref[0] = y.astype(o_ref.dtype)


def bmm_left(f, x, out_dtype, gate=None):
    p, k, c = x.shape
    mo = f.shape[0]
    tc = min(c, 2048)
    xspec = pl.BlockSpec((1, k, tc), lambda i, j: (i, 0, j))
    ospec = pl.BlockSpec((1, mo, tc), lambda i, j: (i, 0, j))
    in_specs = [pl.BlockSpec((mo, k), lambda i, j: (0, 0)), xspec]
    args = [f, x]
    if gate is not None:
        in_specs += [ospec, ospec, pl.BlockSpec((1, tc), lambda i, j: (0, j))]
        args += list(gate)
    return pl.pallas_call(
        functools.partial(_bmm_left_kernel, gated=gate is not None),
        grid=(p, c // tc),
        in_specs=in_specs, out_specs=ospec,
        out_shape=jax.ShapeDtypeStruct((p, mo, c), out_dtype),
        compiler_params=_cparams(("arbitrary", "arbitrary")),
        name="bmm_left_gated" if gate is not None else "bmm_left",
    )(*args)


def _stride_dft_kernel(*refs, gated):
    if gated:
        f_ref, x_ref, g_ref, v_ref, bias_ref, o_ref = refs
    else:
        f_ref, x_ref, o_ref = refs
    td = x_ref.shape[-1]
    x = x_ref[0].reshape(-1, td).astype(BF16)
    y = _dot(f_ref[...], x).reshape(o_ref.shape[1:])
    if gated:
        y = g_ref[0].astype(F32) * (y + bias_ref[...] * v_ref[0].astype(F32))
    o_ref[0] = y.astype(o_ref.dtype)


def stride_dft(fk, x, out_dtype, gate=None):
    p, k, r, d = x.shape
    sub = FFT_SUB
    mo = fk.shape[0] // sub
    td = min(d, 1024)
    xspec = pl.BlockSpec((1, k, sub, td), lambda i, j, l: (i, 0, j, l))
    ospec = pl.BlockSpec((1, mo, sub, td), lambda i, j, l: (i, 0, j, l))
    in_specs = [pl.BlockSpec(fk.shape, lambda i, j, l: (0, 0)), xspec]
    args = [fk, x]
    if gate is not None:
        in_specs += [ospec, ospec, pl.BlockSpec((1, td), lambda i, j, l: (0, l))]
        args += list(gate)
    return pl.pallas_call(
        functools.partial(_stride_dft_kernel, gated=gate is not None),
        grid=(p, r // sub, d // td),
        in_specs=in_specs, out_specs=ospec,
        out_shape=jax.ShapeDtypeStruct((p, mo, r, d), out_dtype),
        compiler_params=_cparams(("arbitrary", "arbitrary", "arbitrary")),
        name="stride_dft_gated" if gate is not None else "stride_dft",
    )(*args)


def _slab_fwd_kernel(m1_ref, t_ref, o_ref):
    r = FFT_SLAB
    t = t_ref[0, :, 0].reshape(2 * r, t_ref.shape[-1]).astype(BF16)
    x = _dot(m1_ref[0], t)
    o_ref[:, 0] = x.reshape(2, r, x.shape[-1])


def slab_spectrum(m1, t):
    _, a, r, d = t.shape
    td = min(d, 2048)
    return pl.pallas_call(
        _slab_fwd_kernel, grid=(a, d // td),
        in_specs=[pl.BlockSpec((1, 2 * r, 2 * r), lambda i, j: (i, 0, 0)),
                  pl.BlockSpec((1, 2, 1, r, td), lambda i, j: (0, 0, i, 0, j))],
        out_specs=pl.BlockSpec((2, 1, r, td), lambda i, j: (0, i, 0, j)),
        out_shape=jax.ShapeDtypeStruct((2, a, r, d), F32),
        compiler_params=_cparams(("arbitrary", "arbitrary")),
        name="slab_spectrum",
    )(m1, t.reshape(1, 2, a, r, d))


def _slab_conv_kernel(m1_ref, m2_ref, h_ref, t_ref, o_ref):
    r = FFT_SLAB
    td = t_ref.shape[-1]
    t = t_ref[0, :, 0].reshape(2 * r, td)
    x = _dot(m1_ref[0], t)
    xr, xi = x[:r], x[r:]
    hr, hi = h_ref[0, 0], h_ref[1, 0]
    y = jnp.concatenate([xr * hr - xi * hi, xr * hi + xi * hr], axis=0).astype(BF16)
    u = _dot(m2_ref[0], y)
    o_ref[0, :, 0] = u.reshape(2, r, td).astype(o_ref.dtype)


def slab_conv(m1, m2, h, t):
    p, _, a, r, d = t.shape
    td = min(d, 2048)
    mspec = pl.BlockSpec((1, 2 * r, 2 * r), lambda i, j, k: (i, 0, 0))
    tspec = pl.BlockSpec((1, 2, 1, r, td), lambda i, j, k: (k, 0, i, 0, j))
    return pl.pallas_call(
        _slab_conv_kernel, grid=(a, d // td, p),
        in_specs=[mspec, mspec, pl.BlockSpec((2, 1, r, td), lambda i, j, k: (0, i, 0, j)), tspec],
        out_specs=tspec,
        out_shape=jax.ShapeDtypeStruct(t.shape, BF16),
        compiler_params=_cparams(("arbitrary", "arbitrary", "arbitrary")),
        name="slab_conv",
    )(m1, m2, h, t)


def _direct_conv_kernel(f1_ref, f2_ref, h_ref, x_ref, g_ref, bias_ref, o_ref):
    xin = x_ref[0]
    half = f1_ref.shape[0] // 2
    x = _dot(f1_ref[...], xin)
    xr, xi = x[:half], x[half:]
    hr, hi = h_ref[0], h_ref[1]
    y = jnp.concatenate([xr * hr - xi * hi, xr * hi + xi * hr], axis=0).astype(BF16)
    u = _dot(f2_ref[...], y)
    o_ref[0] = (g_ref[0].astype(F32) * (u + bias_ref[...] * xin.astype(F32))).astype(o_ref.dtype)


def direct_conv(f1, f2, h, x, g, bias_row):
    p, n2, d = x.shape
    td = 512
    xspec = pl.BlockSpec((1, n2, td), lambda i, j: (i, 0, j))
    return pl.pallas_call(
        _direct_conv_kernel, grid=(p, d // td),
        in_specs=[pl.BlockSpec(f1.shape, lambda i, j: (0, 0)),
                  pl.BlockSpec(f2.shape, lambda i, j: (0, 0)),
                  pl.BlockSpec((2, n2, td), lambda i, j: (0, 0, j)),
                  xspec, xspec, pl.BlockSpec((1, td), lambda i, j: (0, j))],
        out_specs=xspec,
        out_shape=jax.ShapeDtypeStruct((p, n2, d), BF16),
        compiler_params=_cparams(("arbitrary", "arbitrary")),
        name="direct_conv",
    )(f1, f2, h, x, g, bias_row)


def _cplx_real_form(c):
    return np.block([[c.real, -c.imag], [c.imag, c.real]])


def _dft_tables(n):
    nn = 2 * n
    r = FFT_SLAB
    if nn <= 4 * r:
        k = np.arange(nn)[:, None]
        m = np.arange(nn)[None, :]
        fc = np.exp(-2j * np.pi * ((k * m) % nn) / nn)
        f1 = _cplx_real_form(fc[:, :n])
        f2 = _cplx_real_form(np.conj(fc.T)[:n, :] / nn)
        fh = np.concatenate([fc.real, fc.imag], axis=0)
        return dict(f1=jnp.asarray(f1, BF16), f2=jnp.asarray(f2, BF16), fh=jnp.asarray(fh, BF16))
    a = nn // r
    ka = np.arange(a)[:, None]
    aa = np.arange(a)[None, :]
    fa = np.exp(-2j * np.pi * ((ka * aa) % a) / a)
    fa_fwd = _cplx_real_form(fa[:, :a // 2])
    fa_inv = _cplx_real_form(np.conj(fa.T)[:a // 2, :] / nn)
    fa_h = np.concatenate([fa.real, fa.imag], axis=0)
    kb = np.arange(r)[None, :, None]
    bb = np.arange(r)[None, None, :]
    kk = np.arange(a)[:, None, None]
    m1c = np.exp(-2j * np.pi * ((bb * (kk + a * kb)) % nn) / nn)
    m1 = np.stack([_cplx_real_form(m1c[i]) for i in range(a)])
    m2 = np.stack([_cplx_real_form(np.conj(m1c[i].T)) for i in range(a)])
    eye = np.eye(FFT_SUB)
    kron = lambda f: jnp.asarray(np.kron(f, eye), BF16)
    return dict(fa_fwd=kron(fa_fwd), fa_inv=kron(fa_inv), fa_h=kron(fa_h),
                m1=jnp.asarray(m1, BF16), m2=jnp.asarray(m2, BF16))


def hyena_long_convs(x1, x2, v, hfull, f_bias):
    b, n, d = v.shape
    nn = 2 * n
    r = FFT_SLAB
    tabs = _dft_tables(n)
    p = b // 2
    bias = f_bias.astype(F32)
    if 'f1' in tabs:
        h = bmm_left(tabs['fh'], hfull, F32).reshape(HYENA_ORDER, 2, nn, d)
        pair = lambda t: t.reshape(p, nn, d)
        z = direct_conv(tabs['f1'], tabs['f2'], h[0], pair(v), pair(x1), bias[0].reshape(1, d))
        y = direct_conv(tabs['f1'], tabs['f2'], h[1], z, pair(x2), bias[1].reshape(1, d))
        return y.reshape(b, n, d)
    a = nn // r
    th = stride_dft(tabs['fa_h'], hfull.reshape(HYENA_ORDER, a, r, d), BF16)
    h = [slab_spectrum(tabs['m1'], th[o].reshape(2, a, r, d)) for o in range(HYENA_ORDER)]
    pair = lambda t: t.reshape(p, a, r, d)

    def conv(sig, gate, o):
        t = stride_dft(tabs['fa_fwd'], pair(sig), BF16)
        u = slab_conv(tabs['m1'], tabs['m2'], h[o], t.reshape(p, 2, a, r, d))
        return stride_dft(tabs['fa_inv'], u.reshape(p, 2 * a, r, d), BF16,
                          gate=(pair(gate), pair(sig), bias[o].reshape(1, d)))

    z = conv(v, x1, 0)
    y = conv(z.reshape(b, n, d), x2, 1)
    return y.reshape(b, n, d)


def hyena_mixer(u_lat, u_ctx, w_in, conv_w, conv_b, f_w1, f_b1, f_w2, f_b2, f_w3, f_b3, f_w4,
                f_freq, f_bias, ctx_out):
    d = D_MODEL
    w = w_in.astype(BF16)

    def run(u):
        n = u.shape[1]
        hfull = hyena_filters(n, f_w1, f_b1, f_w2, f_b2, f_w3, f_b3, f_w4, f_freq)
        xin = matmul(u, w, BF16, 512)
        x1, x2, v = [dwconv(xin, conv_w[:, i * d:(i + 1) * d], conv_b[i * d:(i + 1) * d], i * d, d, act=False)
                     for i in range(3)]
        return hyena_long_convs(x1, x2, v, hfull, f_bias)

    return run(u_lat), (run(u_ctx) if ctx_out else None)


def kernel(x, c, ctx, c_ctx, ada_w, ada_b, ln_g, ln_b, mlp_w1, mlp_w2, ssd_w_in, ssd_conv_w, ssd_conv_b, ssd_dt_bias, ssd_a_log, ssd_d, ssd_norm_g, ssd_w_out, attn_w_qkv, attn_sink, attn_w_o, hy_w_in, hy_conv_w, hy_conv_b, hy_f_w1, hy_f_b1, hy_f_w2, hy_f_b2, hy_f_w3, hy_f_b3, hy_f_w4, hy_f_freq, hy_f_bias, hy_w_out):
    bsz, _, d = x.shape
    depth = ada_w.shape[0]
    assert bsz + 1 <= COND_ROWS and bsz % 2 == 0
    cond = jnp.zeros((COND_ROWS, d), F32).at[:bsz].set(c.astype(F32)).at[bsz].set(c_ctx.astype(F32))
    mods = ada_mods(cond, ada_w.astype(F32), ada_b.astype(F32))

    def mod_l(i, k):
        return mods[i, :bsz, k * d:(k + 1) * d][:, None, :]

    def mod_c(i, k):
        return jnp.broadcast_to(mods[i, bsz, k * d:(k + 1) * d][None, None, :], (bsz, 1, d))

    zero = jnp.zeros((bsz, 1, d), F32)
    xl, xc = x.astype(F32), ctx.astype(F32)
    ul = modulate(xl, mod_l(0, 1), mod_l(0, 0))
    uc = modulate(xc, mod_c(0, 1), mod_c(0, 0))
    for i in range(depth):
        last = i == depth - 1
        kind = MIXER_OF_LAYER[i]
        j = MIXER_OF_LAYER[:i].count(kind)
        if kind == 0:
            yl, yc = ssd_mixer(ul, uc, ssd_w_in[j], ssd_conv_w[j].astype(F32), ssd_conv_b[j].astype(F32),
                               ssd_dt_bias[j], ssd_a_log[j], ssd_d[j], ssd_norm_g[j], not last)
            w_out = ssd_w_out[j]
        elif kind == 1:
            yl, yc = attn_mixer(ul, uc, attn_w_qkv[j], attn_sink[j], not last)
            w_out = attn_w_o[j]
        else:
            yl, yc = hyena_mixer(ul, uc, hy_w_in[j], hy_conv_w[j].astype(F32), hy_conv_b[j].astype(F32),
                                 hy_f_w1[j], hy_f_b1[j], hy_f_w2[j], hy_f_b2[j], hy_f_w3[j], hy_f_b3[j],
                                 hy_f_w4[j], hy_f_freq[j], hy_f_bias[j], not last)
            w_out = hy_w_out[j]
        w_out = w_out.astype(BF16)
        w1 = mlp_w1[i].astype(BF16)
        w2 = mlp_w2[i].astype(BF16)
        nsc_l, nsh_l = (zero, zero) if last else (mod_l(i + 1, 1), mod_l(i + 1, 0))
        xl, ul = matmul_res_ln(yl, w_out, xl, mod_l(i, 2), ln_g[i, 0], ln_b[i, 0], mod_l(i, 4), mod_l(i, 3))
        xl, ul = mlp_res_ln(ul, w1, w2, xl, mod_l(i, 5), ln_g[i, 1], ln_b[i, 1], nsc_l, nsh_l)
        if not last:
            xc, uc = matmul_res_ln(yc, w_out, xc, mod_c(i, 2), ln_g[i, 0], ln_b[i, 0], mod_c(i, 4), mod_c(i, 3))
            xc, uc = mlp_res_ln(uc, w1, w2, xc, mod_c(i, 5), ln_g[i, 1], ln_b[i, 1],
                                mod_c(i + 1, 1), mod_c(i + 1, 0))
    return xl.astype(x.dtype)
```

```python
import functools
import math

import numpy as np
import jax
import jax.numpy as jnp
from jax import lax
from jax.experimental import pallas as pl
from jax.experimental.pallas import tpu as pltpu

F32 = jnp.float32
BF16 = jnp.bfloat16
HIGHEST = lax.Precision.HIGHEST

D_MODEL = 2048
DEPTH = 4
GRID_W = 64
N_MIXERS = 3
MIXER_OF_LAYER = tuple(i % N_MIXERS for i in range(DEPTH))
ALPHA = (2.0 * DEPTH) ** 0.25
LN_EPS = 1e-5
RMS_EPS = 1e-5
N_MOD = 6
MLP_HIDDEN = 4 * D_MODEL

D_INNER = 2 * D_MODEL
SSM_HEAD_DIM = 64
SSM_HEADS = D_INNER // SSM_HEAD_DIM
SSM_GROUPS = 8
SSM_HPG = SSM_HEADS // SSM_GROUPS
SSM_STATE = 128
SSM_CONV_W = 5
SSM_CHUNK = 128
SSM_XBC = D_INNER + 2 * SSM_GROUPS * SSM_STATE
SSM_GW = D_INNER // SSM_GROUPS
SSM_SPLIT_ROWS = 6 * SSM_HPG
SSM_GPS = 8
LOG2E = math.log2(math.e)

ATTN_HEAD_DIM = 64
ATTN_Q_HEADS = D_MODEL // ATTN_HEAD_DIM
ATTN_KV_HEADS = 4
ATTN_GROUP = ATTN_Q_HEADS // ATTN_KV_HEADS
ATTN_Q_W = ATTN_Q_HEADS * ATTN_HEAD_DIM
ATTN_KV_W = ATTN_KV_HEADS * ATTN_HEAD_DIM
WINDOW = 128
ATTN_BLOCK = 128
ROPE_BASE = 10000.0
ATTN_MASKED = -1e30

HYENA_ORDER = 2
HYENA_SHORT_W = 3
HYENA_EMB = 33
HYENA_FILTER_W = 64
HYENA_DECAY_FAST = 0.3
HYENA_DECAY_SLOW = 1.5
HYENA_DECAY_TARGET = 1e-2

LANE = 128
FFT_SLAB = 128
FFT_SUB = 16
COND_ROWS = 16
VMEM_LIMIT_MB = 56


def _cparams(sem, vmem_mb=VMEM_LIMIT_MB):
    return pltpu.CompilerParams(dimension_semantics=sem,
                                vmem_limit_bytes=vmem_mb * 1024 * 1024)


def _dot(a, b, precision=None):
    return jnp.dot(a, b, preferred_element_type=F32, precision=precision)


def _dot_nt(a, b):
    return lax.dot_general(a, b, (((1,), (1,)), ((), ())), preferred_element_type=F32)


def _dot_tn(a, b):
    return lax.dot_general(a, b, (((0,), (0,)), ((), ())), preferred_element_type=F32)


def _silu(x):
    return x * jax.nn.sigmoid(x)


def _split_parts(v, parts):
    out = []
    for _ in range(parts):
        piece = v.astype(BF16).astype(F32)
        out.append(piece)
        v = v - piece
    return out


def _ada_kernel(c_ref, w_ref, b_ref, o_ref):
    s = _silu(c_ref[...])
    o_ref[0] = _dot(s, w_ref[0], HIGHEST) + b_ref[0]


def ada_mods(cond, ada_w, ada_b):
    depth, d, n6 = ada_w.shape
    tn = 512
    return pl.pallas_call(
        _ada_kernel,
        grid=(depth, n6 // tn),
        in_specs=[pl.BlockSpec((COND_ROWS, d), lambda l, j: (0, 0)),
                  pl.BlockSpec((1, d, tn), lambda l, j: (l, 0, j)),
                  pl.BlockSpec((1, 1, tn), lambda l, j: (l, 0, j))],
        out_specs=pl.BlockSpec((1, COND_ROWS, tn), lambda l, j: (l, 0, j)),
        out_shape=jax.ShapeDtypeStruct((depth, COND_ROWS, n6), F32),
        compiler_params=_cparams(("arbitrary", "arbitrary")),
        name="ada_mods",
    )(cond, ada_w, ada_b.reshape(depth, 1, n6))


def _modulate_kernel(x_ref, sc_ref, sh_ref, o_ref):
    o_ref[0] = (x_ref[0] * (1.0 + sc_ref[0]) + sh_ref[0]).astype(o_ref.dtype)


def modulate(x, sc, sh):
    b, n, d = x.shape
    tm = min(n, 1024)
    row = pl.BlockSpec((1, tm, d), lambda i, j: (i, j, 0))
    vec = pl.BlockSpec((1, 1, d), lambda i, j: (i, 0, 0))
    return pl.pallas_call(
        _modulate_kernel, grid=(b, n // tm),
        in_specs=[row, vec, vec], out_specs=row,
        out_shape=jax.ShapeDtypeStruct((b, n, d), BF16),
        compiler_params=_cparams(("arbitrary", "arbitrary")),
        name="modulate",
    )(x, sc, sh)


def _mm_kernel(a_ref, w_ref, o_ref):
    o_ref[0] = _dot(a_ref[0], w_ref[...]).astype(o_ref.dtype)


def matmul(a, w, out_dtype, tn):
    b, n, k = a.shape
    nn = w.shape[1]
    tm = min(n, 1024)
    tn = min(tn, nn)
    return pl.pallas_call(
        _mm_kernel, grid=(b, n // tm, nn // tn),
        in_specs=[pl.BlockSpec((1, tm, k), lambda i, j, l: (i, j, 0)),
                  pl.BlockSpec((k, tn), lambda i, j, l: (0, l))],
        out_specs=pl.BlockSpec((1, tm, tn), lambda i, j, l: (i, j, l)),
        out_shape=jax.ShapeDtypeStruct((b, n, nn), out_dtype),
        compiler_params=_cparams(("arbitrary", "arbitrary", "arbitrary")),
        name="matmul",
    )(a, w)


def _res_ln(x, y, gate, g, b):
    h = ALPHA * x + gate * y
    mu = jnp.mean(h, axis=-1, keepdims=True)
    hc = h - mu
    var = jnp.mean(hc * hc, axis=-1, keepdims=True)
    return hc * lax.rsqrt(var + LN_EPS) * g + b


def _mm_res_ln_kernel(a_ref, w_ref, x_ref, gate_ref, g_ref, b_ref, sc_ref, sh_ref, xo_ref, uo_ref):
    xn = _res_ln(x_ref[0], _dot(a_ref[0], w_ref[...]), gate_ref[0], g_ref[...], b_ref[...])
    xo_ref[0] = xn
    uo_ref[0] = (xn * (1.0 + sc_ref[0]) + sh_ref[0]).astype(uo_ref.dtype)


def matmul_res_ln(a, w, x, gate, ln_g, ln_b, sc_next, sh_next):
    b, n, kk = a.shape
    d = w.shape[1]
    tm = min(n, 512)
    row = pl.BlockSpec((1, tm, d), lambda i, j: (i, j, 0))
    vec = pl.BlockSpec((1, 1, d), lambda i, j: (i, 0, 0))
    par = pl.BlockSpec((1, d), lambda i, j: (0, 0))
    return pl.pallas_call(
        _mm_res_ln_kernel, grid=(b, n // tm),
        in_specs=[pl.BlockSpec((1, tm, kk), lambda i, j: (i, j, 0)),
                  pl.BlockSpec((kk, d), lambda i, j: (0, 0), pipeline_mode=pl.Buffered(1)),
                  row, vec, par, par, vec, vec],
        out_specs=[row, row],
        out_shape=[jax.ShapeDtypeStruct((b, n, d), F32), jax.ShapeDtypeStruct((b, n, d), BF16)],
        compiler_params=_cparams(("arbitrary", "arbitrary")),
        name="matmul_res_ln",
    )(a, w, x, gate, ln_g.reshape(1, d), ln_b.reshape(1, d), sc_next, sh_next)


def _mlp_kernel(u_ref, w1_ref, w2_ref, x_ref, gate_ref, g_ref, b_ref, sc_ref, sh_ref,
                xo_ref, uo_ref, acc_ref):
    k = pl.program_id(2)

    @pl.when(k == 0)
    def _():
        acc_ref[...] = jnp.zeros_like(acc_ref)

    h = jnp.maximum(_dot(u_ref[0], w1_ref[...]), 0.0)
    acc_ref[...] += _dot((h * h).astype(BF16), w2_ref[...])

    @pl.when(k == pl.num_programs(2) - 1)
    def _():
        xn = _res_ln(x_ref[0], acc_ref[...], gate_ref[0], g_ref[...], b_ref[...])
        xo_ref[0] = xn
        uo_ref[0] = (xn * (1.0 + sc_ref[0]) + sh_ref[0]).astype(uo_ref.dtype)


def mlp_res_ln(u, w1, w2, x, gate, ln_g, ln_b, sc_next, sh_next):
    b, n, d = u.shape
    hid = w1.shape[1]
    tm = min(n, 512)
    th = 1024
    row = pl.BlockSpec((1, tm, d), lambda i, j, k: (i, j, 0))
    vec = pl.BlockSpec((1, 1, d), lambda i, j, k: (i, 0, 0))
    par = pl.BlockSpec((1, d), lambda i, j, k: (0, 0))
    return pl.pallas_call(
        _mlp_kernel, grid=(b, n // tm, hid // th),
        in_specs=[row,
                  pl.BlockSpec((d, th), lambda i, j, k: (0, k)),
                  pl.BlockSpec((th, d), lambda i, j, k: (k, 0)),
                  row, vec, par, par, vec, vec],
        out_specs=[row, row],
        out_shape=[jax.ShapeDtypeStruct((b, n, d), F32), jax.ShapeDtypeStruct((b, n, d), BF16)],
        scratch_shapes=[pltpu.VMEM((tm, d), F32)],
        compiler_params=_cparams(("arbitrary", "arbitrary", "arbitrary")),
        name="mlp_res_ln",
    )(u, w1, w2, x, gate, ln_g.reshape(1, d), ln_b.reshape(1, d), sc_next, sh_next)


def _dwconv_kernel(x_ref, w_ref, b_ref, o_ref, *, width, act):
    x = x_ref[0].astype(F32)
    n = x.shape[0]
    half = width // 2
    rows = lax.broadcasted_iota(jnp.int32, x.shape, 0)
    acc = x * w_ref[half:half + 1, :] + b_ref[...]
    for k in range(width):
        off = k - half
        if off == 0:
            continue
        shifted = pltpu.roll(x, (-off) % n, axis=0)
        valid = (rows + off >= 0) & (rows + off < n)
        acc = acc + jnp.where(valid, shifted, 0.0) * w_ref[k:k + 1, :]
    if act:
        acc = _silu(acc)
    o_ref[0] = acc.astype(o_ref.dtype)


def dwconv(x, w, bias, col0, ncols, act, out_dtype=BF16):
    b, n, _ = x.shape
    width = w.shape[0]
    tc = 256 if n > 1024 else 512
    tc = min(tc, ncols)
    cb0 = col0 // tc
    return pl.pallas_call(
        functools.partial(_dwconv_kernel, width=width, act=act),
        grid=(b, ncols // tc),
        in_specs=[pl.BlockSpec((1, n, tc), lambda i, j: (i, 0, cb0 + j)),
                  pl.BlockSpec((width, tc), lambda i, j: (0, j)),
                  pl.BlockSpec((1, tc), lambda i, j: (0, j))],
        out_specs=pl.BlockSpec((1, n, tc), lambda i, j: (i, 0, j)),
        out_shape=jax.ShapeDtypeStruct((b, n, ncols), out_dtype),
        compiler_params=_cparams(("arbitrary", "arbitrary")),
        name="dwconv",
    )(x, w, bias.reshape(1, ncols))


def _dtprep_kernel(raw_ref, bias_ref, a_ref, at_ref, r0_ref, ft_ref):
    x = raw_ref[0] + bias_ref[...]
    dt = jnp.maximum(x, 0.0) + jnp.log1p(jnp.exp(-jnp.abs(x)))
    adt = dt * a_ref[...]
    q = x.shape[0]
    r = lax.broadcasted_iota(jnp.int32, (q, q), 0)
    c = lax.broadcasted_iota(jnp.int32, (q, q), 1)
    fwd = _dot((c <= r).astype(F32), adt, HIGHEST)
    bwd = _dot((c >= r).astype(F32), adt, HIGHEST)
    lane = lax.broadcasted_iota(jnp.int32, x.shape, 1)
    cum = jnp.where(lane < SSM_HEADS, fwd, bwd)
    tot = jnp.where(lane[0:1] < SSM_HEADS, cum[q - 1:q], cum[0:1])
    c3 = _split_parts((cum * LOG2E).T, 3)
    fparts = [s for v in (dt, jnp.exp(cum), jnp.exp(tot - cum) * dt) for s in _split_parts(v.T, 2)]
    ones = jnp.ones((3 * SSM_HPG, q), F32)
    for d in range(2):
        for g in range(SSM_GROUPS):
            rows = slice(d * SSM_HEADS + g * SSM_HPG, d * SSM_HEADS + (g + 1) * SSM_HPG)
            at_ref[0, 0, d, g] = jnp.concatenate([s[rows] for s in c3] + [ones], axis=0).astype(BF16)
            r0_ref[0, 0, d, g] = jnp.concatenate([ones] + [-s[rows] for s in c3], axis=0).astype(BF16)
            ft_ref[0, 0, d, g] = jnp.concatenate([s[rows] for s in fparts], axis=0).astype(BF16)


def ssd_dtprep(dt_raw, dt_bias, a_log):
    b, n, w = dt_raw.shape
    q = SSM_CHUNK
    nc = n // q
    a = -jnp.exp(a_log.astype(F32)).reshape(1, w)
    par = pl.BlockSpec((1, w), lambda i, j: (0, 0))
    oshape = (b, nc, 2, SSM_GROUPS, SSM_SPLIT_ROWS, q)
    ospec = pl.BlockSpec((1, 1) + oshape[2:], lambda i, j: (i, j, 0, 0, 0, 0))
    return pl.pallas_call(
        _dtprep_kernel, grid=(b, nc),
        in_specs=[pl.BlockSpec((1, q, w), lambda i, j: (i, j, 0)), par, par],
        out_specs=[ospec] * 3,
        out_shape=[jax.ShapeDtypeStruct(oshape, BF16)] * 3,
        compiler_params=_cparams(("arbitrary", "arbitrary")),
        name="ssd_dtprep",
    )(dt_raw, dt_bias.reshape(1, w).astype(F32), a)


def _ssd_scan_kernel(*refs, reverse, finish):
    if finish:
        (x_ref, b_ref, c_ref, at_ref, r0_ref, ft_ref, bmask_ref, eexp_ref, h0_ref,
         yf_ref, z_ref, dsk_ref, ng_ref, y_ref, hout_ref, h_scr) = refs
    else:
        (x_ref, b_ref, c_ref, at_ref, r0_ref, ft_ref, bmask_ref, eexp_ref, h0_ref,
         y_ref, hout_ref, h_scr) = refs
    step = pl.program_id(2)

    @pl.when(step == 0)
    def _():
        h_scr[...] = h0_ref[0]

    q = SSM_CHUNK
    p = SSM_HEAD_DIM
    gw = SSM_GW
    ns = SSM_STATE
    ri = lax.broadcasted_iota(jnp.int32, (q, q), 0)
    ci = lax.broadcasted_iota(jnp.int32, (q, q), 1)
    keep = (ci >= ri) if reverse else (ci <= ri)
    left = lax.broadcasted_iota(jnp.int32, (q, 2 * p), 1) < p

    for gi in range(SSM_GPS):
        gcols = slice(gi * gw, (gi + 1) * gw)
        x = x_ref[0, :, gcols].astype(F32)
        bm = b_ref[0, :, gi * ns:(gi + 1) * ns]
        cm = c_ref[0, :, gi * ns:(gi + 1) * ns]
        cb = jnp.where(keep, _dot_nt(cm, bm), 0.0)
        rhs = jnp.concatenate([r0_ref[0, 0, 0, gi]] * SSM_HPG, axis=1) * bmask_ref[...]
        diff = _dot_tn(at_ref[0, 0, 0, gi], rhs)
        ft = ft_ref[0, 0, 0, gi]
        fac = _dot_tn(ft, eexp_ref[...])
        ec = fac[:, :gw]
        xw = (x * fac[:, gw:]).astype(BF16)
        etot = ec[0:1] if reverse else ec[q - 1:q]
        ht = h_scr[gi]
        yoff = _dot(cm, ht.astype(BF16)) * ec
        dtj = ft[0:SSM_HPG].astype(F32) + ft[SSM_HPG:2 * SSM_HPG].astype(F32)

        ys = []
        for pr in range(SSM_HPG // 2):
            cols = slice(pr * 2 * p, (pr + 1) * 2 * p)
            xp = x[:, cols]
            yp = yoff[:, cols]
            for hd, xm in ((2 * pr, jnp.where(left, xp, 0.0)), (2 * pr + 1, jnp.where(left, 0.0, xp))):
                dec = jnp.exp2(jnp.minimum(diff[:, hd * q:(hd + 1) * q], 0.0))
                yp = yp + _dot((cb * dtj[hd:hd + 1] * dec).astype(BF16), xm.astype(BF16))
            ys.append(yp)
        y = jnp.concatenate(ys, axis=1)
        h_scr[gi] = ht * etot + _dot_tn(bm, xw)

        if finish:
            yt = y + yf_ref[0, :, gcols] + dsk_ref[:, gcols] * x
            yt = yt * _silu(z_ref[0, :, gcols].astype(F32))
            ms = jnp.mean(yt * yt, axis=-1, keepdims=True)
            y_ref[0, :, gcols] = (yt * lax.rsqrt(ms + RMS_EPS) * ng_ref[:, gcols]).astype(y_ref.dtype)
        else:
            y_ref[0, :, gcols] = y

    @pl.when(step == pl.num_programs(2) - 1)
    def _():
        hout_ref[0] = h_scr[...]


def _ssd_tables():
    q, hg, p = SSM_CHUNK, SSM_HPG, SSM_HEAD_DIM
    r = np.arange(SSM_SPLIT_ROWS)[:, None]
    bmask = (r % hg == np.arange(hg * q)[None, :] // q)
    c = np.arange(2 * SSM_GW)[None, :]
    eexp = (r // (2 * hg) == 1 + c // SSM_GW) & (r % hg == (c % SSM_GW) // p)
    return jnp.asarray(bmask, BF16), jnp.asarray(eexp, BF16)


def ssd_scan(xbc, at, r0, ft, h0, *, reverse, fin=None):
    b, n, _ = xbc.shape
    q, g, gw, ns = SSM_CHUNK, SSM_GROUPS, SSM_GW, SSM_STATE
    sr = SSM_SPLIT_ROWS
    nc = n // q
    bmask, eexp = _ssd_tables()

    def ch(k):
        return nc - 1 - k if reverse else k

    gps = SSM_GPS
    bcol = D_INNER // (gps * ns)
    direction = 1 if reverse else 0
    opspec = pl.BlockSpec((1, 1, 1, gps, sr, q), lambda i, j, k: (i, ch(k), direction, j, 0, 0))
    in_specs = [
        pl.BlockSpec((1, q, gps * gw), lambda i, j, k: (i, ch(k), j)),
        pl.BlockSpec((1, q, gps * ns), lambda i, j, k: (i, ch(k), bcol + j)),
        pl.BlockSpec((1, q, gps * ns), lambda i, j, k: (i, ch(k), bcol + g // gps + j)),
        opspec, opspec, opspec,
        pl.BlockSpec(bmask.shape, lambda i, j, k: (0, 0)),
        pl.BlockSpec(eexp.shape, lambda i, j, k: (0, 0)),
        pl.BlockSpec((1, gps, ns, gw), lambda i, j, k: (i, j, 0, 0)),
    ]
    args = [xbc, xbc, xbc, at, r0, ft, bmask, eexp, h0]
    yspec = pl.BlockSpec((1, q, gps * gw), lambda i, j, k: (i, ch(k), j))
    if fin is not None:
        y_other, z, dsk, ng = fin
        in_specs += [yspec, yspec,
                     pl.BlockSpec((1, gps * gw), lambda i, j, k: (0, j)),
                     pl.BlockSpec((1, gps * gw), lambda i, j, k: (0, j))]
        args += [y_other, z, dsk, ng]
    return pl.pallas_call(
        functools.partial(_ssd_scan_kernel, reverse=reverse, finish=fin is not None),
        grid=(b, g // gps, nc),
        in_specs=in_specs,
        out_specs=[yspec, pl.BlockSpec((1, gps, ns, gw), lambda i, j, k: (i, j, 0, 0))],
        out_shape=[jax.ShapeDtypeStruct((b, n, D_INNER), BF16 if fin is not None else F32),
                   jax.ShapeDtypeStruct((b, g, ns, gw), F32)],
        scratch_shapes=[pltpu.VMEM((gps, ns, gw), F32)],
        compiler_params=_cparams(("arbitrary", "arbitrary", "arbitrary")),
        name="ssd_scan_bwd" if reverse else "ssd_scan_fwd",
    )(*args)


def ssd_mixer(u_lat, u_ctx, w_in, conv_w, conv_b, dt_bias, a_log, d_skip, norm_g, ctx_out):
    g, hg = SSM_GROUPS, SSM_HPG
    w_z = w_in[:, :D_INNER].astype(BF16)
    w_xbc = w_in[:, D_INNER:D_INNER + SSM_XBC].astype(BF16)
    w_dt = w_in[:, D_INNER + SSM_XBC:].astype(BF16)
    dsk = jnp.repeat(d_skip.astype(F32), SSM_HEAD_DIM).reshape(1, D_INNER)
    ng = norm_g.astype(F32).reshape(1, D_INNER)

    def project(u):
        b, n, _ = u.shape
        z = matmul(u, w_z, BF16, 512)
        xbc = dwconv(matmul(u, w_xbc, BF16, 512), conv_w, conv_b, 0, SSM_XBC, act=True)
        at, r0, ft = ssd_dtprep(matmul(u, w_dt, F32, 2 * SSM_HEADS), dt_bias, a_log)
        return z, xbc, at, r0, ft

    def bidir(proj, h_f, h_b, want_y):
        z, xbc, at, r0, ft = proj
        y_f, s_f = ssd_scan(xbc, at, r0, ft, h_f, reverse=False)
        fin = (y_f, z, dsk, ng) if want_y else None
        y, s_b = ssd_scan(xbc, at, r0, ft, h_b, reverse=True, fin=fin)
        return y, s_f, s_b

    h0 = jnp.zeros((u_lat.shape[0], g, SSM_STATE, SSM_GW), F32)
    yc, hc_f, hc_b = bidir(project(u_ctx), h0, h0, ctx_out)
    yl, _, _ = bidir(project(u_lat), hc_f, hc_b, True)
    return yl, (yc if ctx_out else None)


def _rope_kernel(x_ref, cos_ref, sin_ref, o_ref):
    cos = cos_ref[...]
    sin = sin_ref[...]
    half = ATTN_HEAD_DIM // 2
    first = (lax.broadcasted_iota(jnp.int32, cos.shape, 1) % ATTN_HEAD_DIM) < half
    for t in range(x_ref.shape[2] // LANE):
        cols = slice(t * LANE, (t + 1) * LANE)
        x = x_ref[0, :, cols].astype(F32)
        partner = jnp.where(first, pltpu.roll(x, LANE - half, axis=1), pltpu.roll(x, half, axis=1))
        o_ref[0, :, cols] = (x * cos + partner * sin).astype(o_ref.dtype)


def rope(qkv, cos_t, sin_t):
    b, n, _ = qkv.shape
    w = ATTN_Q_W + ATTN_KV_W
    tm = 512
    return pl.pallas_call(
        _rope_kernel, grid=(b, n // tm),
        in_specs=[pl.BlockSpec((1, tm, w), lambda i, j: (i, j, 0)),
                  pl.BlockSpec((tm, LANE), lambda i, j: (j, 0)),
                  pl.BlockSpec((tm, LANE), lambda i, j: (j, 0))],
        out_specs=pl.BlockSpec((1, tm, w), lambda i, j: (i, j, 0)),
        out_shape=jax.ShapeDtypeStruct((b, n, w), BF16),
        compiler_params=_cparams(("arbitrary", "arbitrary")),
        name="rope",
    )(qkv, cos_t, sin_t)


def _attn_kernel(*refs, local, seq_len):
    if local:
        (q_ref, k0_ref, k1_ref, k2_ref, v0_ref, v1_ref, v2_ref, kc_ref, vc_ref, sink_ref, band_ref,
         o_ref, qx_ref, kx_ref, vx_ref) = refs
        kv_blocks = ((k0_ref, v0_ref), (k1_ref, v1_ref), (k2_ref, v2_ref))
    else:
        q_ref, kc_ref, vc_ref, sink_ref, o_ref, qx_ref, kx_ref, vx_ref = refs
        kv_blocks = ()
    blk, dh, grp = ATTN_BLOCK, ATTN_HEAD_DIM, ATTN_GROUP
    rows = grp * blk
    nloc = len(kv_blocks) * blk
    pad = LANE - dh

    @pl.when((pl.program_id(0) == 0) & (pl.program_id(1) == 0))
    def _():
        r = lax.broadcasted_iota(jnp.int32, (rows, LANE), 0) % blk
        c = lax.broadcasted_iota(jnp.int32, (rows, LANE), 1)
        cpad = lax.broadcasted_iota(jnp.int32, (rows, pad), 1)
        for kv in range(ATTN_KV_HEADS):
            qx_ref[kv, :, 0:LANE] = jnp.where(r == c, 1.0, 0.0).astype(BF16)
            qx_ref[kv, :, LANE + dh:] = jnp.where(cpad == 0, 1.0, 0.0).astype(BF16)
            kx_ref[kv] = jnp.zeros(kx_ref.shape[1:], BF16)
            if local:
                kx_ref[kv, 0:nloc, 0:LANE] = band_ref[...]
            vx_ref[kv, :, dh:] = jnp.ones((vx_ref.shape[1], pad), BF16)

    if local:
        sj = lax.broadcasted_iota(jnp.int32, (nloc, pad), 0)
        lane = lax.broadcasted_iota(jnp.int32, (nloc, pad), 1)
        kpos = (pl.program_id(1) - 1) * blk + sj
        outside = (kpos < 0) | (kpos >= seq_len)
        edge = jnp.where((lane == 0) & outside, ATTN_MASKED, 0.0).astype(BF16)

    for kv in range(ATTN_KV_HEADS):
        hc = slice(kv * dh, (kv + 1) * dh)
        if local:
            kx_ref[kv, 0:nloc, LANE + dh:] = edge
        for g in range(grp):
            c0 = (kv * grp + g) * dh
            qx_ref[kv, g * blk:(g + 1) * blk, LANE:LANE + dh] = q_ref[0, :, c0:c0 + dh]
        for t, (k_ref, v_ref) in enumerate(kv_blocks):
            kx_ref[kv, t * blk:(t + 1) * blk, LANE:LANE + dh] = k_ref[0, :, hc]
            vx_ref[kv, t * blk:(t + 1) * blk, 0:dh] = v_ref[0, :, hc]
        kx_ref[kv, nloc:, LANE:LANE + dh] = kc_ref[0, :, hc]
        vx_ref[kv, nloc:, 0:dh] = vc_ref[0, :, hc]
    heads = range(ATTN_KV_HEADS)
    s = [_dot_nt(qx_ref[kv], kx_ref[kv]) for kv in heads]
    m = [jnp.maximum(jnp.max(s[kv], axis=-1, keepdims=True), sink_ref[kv]) for kv in heads]
    p = [jnp.exp(s[kv] - m[kv]).astype(BF16) for kv in heads]
    oe = [_dot(p[kv], vx_ref[kv]) for kv in heads]
    for kv in heads:
        den = pltpu.roll(oe[kv], dh, axis=1) + jnp.exp(sink_ref[kv] - m[kv])
        o = (oe[kv] / den).astype(o_ref.dtype)
        for g in range(grp):
            c0 = (kv * grp + g) * dh
            o_ref[0, :, c0:c0 + dh] = o[g * blk:(g + 1) * blk, 0:dh]


def attention(qk, qkv, qkv_ctx, sink_col, *, local):
    b, n, _ = qk.shape
    nctx = qkv_ctx.shape[1]
    blk = ATTN_BLOCK
    nblk = n // blk
    kcb = ATTN_Q_W // ATTN_KV_W
    vcb = kcb + 1
    qspec = pl.BlockSpec((1, blk, ATTN_Q_W), lambda i, j: (i, j, 0))
    ctx_k = pl.BlockSpec((1, nctx, ATTN_KV_W), lambda i, j: (i, 0, kcb))
    ctx_v = pl.BlockSpec((1, nctx, ATTN_KV_W), lambda i, j: (i, 0, vcb))
    sspec = pl.BlockSpec(sink_col.shape, lambda i, j: (0, 0, 0))
    if local:
        def win(cb, off):
            return pl.BlockSpec((1, blk, ATTN_KV_W),
                                lambda i, j: (i, jnp.clip(j + off, 0, nblk - 1), cb))
        sidx = np.arange(3 * blk)[:, None]
        ridx = np.arange(blk)[None, :]
        band = jnp.asarray(np.where(np.abs(sidx - blk - ridx) <= WINDOW, 0.0, ATTN_MASKED), BF16)
        in_specs = [qspec, win(kcb, -1), win(kcb, 0), win(kcb, 1),
                    win(vcb, -1), win(vcb, 0), win(vcb, 1), ctx_k, ctx_v, sspec,
                    pl.BlockSpec(band.shape, lambda i, j: (0, 0))]
        args = [qk, qk, qk, qk, qkv, qkv, qkv, qkv_ctx, qkv_ctx, sink_col, band]
        nkeys = 3 * blk + nctx
    else:
        in_specs = [qspec, ctx_k, ctx_v, sspec]
        args = [qk, qkv_ctx, qkv_ctx, sink_col]
        nkeys = nctx
    return pl.pallas_call(
        functools.partial(_attn_kernel, local=local, seq_len=n),
        grid=(b, nblk),
        in_specs=in_specs,
        out_specs=qspec,
        out_shape=jax.ShapeDtypeStruct((b, n, ATTN_Q_W), BF16),
        scratch_shapes=[pltpu.VMEM((ATTN_KV_HEADS, ATTN_GROUP * blk, 2 * LANE), BF16),
                        pltpu.VMEM((ATTN_KV_HEADS, nkeys, 2 * LANE), BF16),
                        pltpu.VMEM((ATTN_KV_HEADS, nkeys, LANE), BF16)],
        compiler_params=_cparams(("arbitrary", "arbitrary")),
        name="attention_local" if local else "attention_ctx",
    )(*args)


def _rope_tables(n):
    rows = n // GRID_W
    row_id = np.repeat(np.arange(rows, dtype=np.float32), GRID_W)
    col_id = np.tile(np.arange(GRID_W, dtype=np.float32), rows)
    pairs = ATTN_HEAD_DIM // 4
    inv = (np.float32(ROPE_BASE) ** (-np.arange(pairs, dtype=np.float32) / np.float32(pairs))).astype(np.float32)
    ang = np.concatenate([row_id[:, None] * inv, col_id[:, None] * inv], axis=-1).astype(np.float32)
    cos, sin = np.cos(ang.astype(np.float64)), np.sin(ang.astype(np.float64))
    reps = LANE // ATTN_HEAD_DIM
    cos_t = np.tile(np.concatenate([cos, cos], axis=-1), (1, reps))
    sin_t = np.tile(np.concatenate([-sin, sin], axis=-1), (1, reps))
    return jnp.asarray(cos_t, F32), jnp.asarray(sin_t, F32)


def attn_mixer(u_lat, u_ctx, w_qkv, sink, ctx_out):
    dh = ATTN_HEAD_DIM
    perm = np.concatenate([np.arange(0, dh, 2), np.arange(1, dh, 2)])
    nqk = ATTN_Q_HEADS + ATTN_KV_HEADS
    cols = (np.arange(nqk)[:, None] * dh + perm[None, :]).reshape(-1)
    cols = np.concatenate([cols, np.arange(nqk * dh, nqk * dh + ATTN_KV_W)])
    colscale = np.where(np.arange(cols.shape[0]) < ATTN_Q_W, dh ** -0.5, 1.0).astype(np.float32)
    w = (w_qkv[:, cols] * colscale).astype(BF16)
    sink_col = jnp.repeat(sink.astype(F32).reshape(ATTN_KV_HEADS, ATTN_GROUP), ATTN_BLOCK, axis=1)
    sink_col = sink_col.reshape(ATTN_KV_HEADS, ATTN_GROUP * ATTN_BLOCK, 1)

    qkv_l = matmul(u_lat, w, BF16, 512)
    qkv_c = matmul(u_ctx, w, BF16, 512)
    cos_t, sin_t = _rope_tables(u_lat.shape[1])
    qk_l = rope(qkv_l, cos_t, sin_t)
    o_l = attention(qk_l, qkv_l, qkv_c, sink_col, local=True)
    o_c = attention(qkv_c, qkv_c, qkv_c, sink_col, local=False) if ctx_out else None
    return o_l, o_c


def _filter_kernel(z_ref, t_ref, w1_ref, b1_ref, w2_ref, b2_ref, w3_ref, b3_ref, fr_ref,
                   w4f_ref, w4b_ref, dl_ref, o_ref, h_scr, *, seq_len):
    @pl.when((pl.program_id(1) == 0) & (pl.program_id(2) == 0))
    def _():
        fr = fr_ref[...]
        h = jnp.sin(fr * (_dot(z_ref[...], w1_ref[...], HIGHEST) + b1_ref[...]))
        h = jnp.sin(fr * (_dot(h, w2_ref[...], HIGHEST) + b2_ref[...]))
        h_scr[...] = jnp.sin(fr * (_dot(h, w3_ref[...], HIGHEST) + b3_ref[...]))

    h = h_scr[...]
    tr = h.shape[0]
    r = pl.program_id(0) * tr + lax.broadcasted_iota(jnp.int32, (tr, 1), 0)
    hf = _dot(h, w4f_ref[...], HIGHEST)
    hb = _dot(h, w4b_ref[...], HIGHEST)
    val = jnp.where(r < seq_len, hf, hb) * jnp.exp(-t_ref[...] * dl_ref[...])
    o_ref[0] = jnp.where(r == seq_len, 0.0, val)


def hyena_filters(n, w1, b1, w2, b2, w3, b3, w4, freq):
    d = D_MODEL
    fw = HYENA_FILTER_W
    lag = np.concatenate([np.arange(n), [0], np.arange(n - 1, 0, -1)])
    t = np.linspace(0.0, 1.0, n, dtype=np.float32)[:, None]
    bands = (HYENA_EMB - 1) // 2
    wv = (np.float32(2.0 * math.pi) * np.arange(n, dtype=np.float32)[:, None] / np.float32(n)).astype(np.float32)
    f = np.linspace(1e-4, bands - 1, bands, dtype=np.float32)[None, :]
    fwv = (f * wv).astype(np.float32).astype(np.float64)
    z = np.concatenate([t, np.cos(fwv), -np.sin(fwv)], axis=-1).astype(np.float32)
    zpad = np.zeros((2 * n, fw), np.float32)
    zpad[:, :HYENA_EMB] = z[lag]
    tfull = t[lag]
    max_decay = math.log(HYENA_DECAY_TARGET) / HYENA_DECAY_FAST
    min_decay = math.log(HYENA_DECAY_TARGET) / HYENA_DECAY_SLOW
    deltas = np.abs(np.linspace(min_decay, max_decay, d, dtype=np.float32))[None, :]
    w1p = jnp.zeros((fw, fw), F32).at[:HYENA_EMB].set(w1.astype(F32))
    tr = min(2 * n, 512)
    td = 512
    nd = d // td
    par = lambda shape: pl.BlockSpec(shape, lambda i, o, j: (0, 0))
    return pl.pallas_call(
        functools.partial(_filter_kernel, seq_len=n),
        grid=(2 * n // tr, HYENA_ORDER, nd),
        in_specs=[pl.BlockSpec((tr, fw), lambda i, o, j: (i, 0)),
                  pl.BlockSpec((tr, 1), lambda i, o, j: (i, 0)),
                  par((fw, fw)), par((1, fw)), par((fw, fw)), par((1, fw)),
                  par((fw, fw)), par((1, fw)), par((1, fw)),
                  pl.BlockSpec((fw, td), lambda i, o, j: (0, (2 * o) * nd + j)),
                  pl.BlockSpec((fw, td), lambda i, o, j: (0, (2 * o + 1) * nd + j)),
                  pl.BlockSpec((1, td), lambda i, o, j: (0, j))],
        out_specs=pl.BlockSpec((1, tr, td), lambda i, o, j: (o, i, j)),
        out_shape=jax.ShapeDtypeStruct((HYENA_ORDER, 2 * n, d), F32),
        scratch_shapes=[pltpu.VMEM((tr, fw), F32)],
        compiler_params=_cparams(("arbitrary", "arbitrary", "arbitrary")),
        name="hyena_filters",
    )(jnp.asarray(zpad), jnp.asarray(tfull), w1p, b1.astype(F32).reshape(1, fw),
      w2.astype(F32), b2.astype(F32).reshape(1, fw), w3.astype(F32), b3.astype(F32).reshape(1, fw),
      freq.astype(F32).reshape(1, fw), w4.astype(F32), w4.astype(F32), jnp.asarray(deltas))


def _bmm_left_kernel(*refs, gated):
    if gated:
        f_ref, x_ref, g_ref, v_ref, bias_ref, o_ref = refs
    else:
        f_ref, x_ref, o_ref = refs
    y = _dot(f_ref[...], x_ref[0].astype(BF16))
    if gated:
        v = v_ref[0].astype(F32)
        y = g_ref[0].astype(F32) * (y + bias_ref[...] * v)
    o_ref[0] = y.astype(o_ref.dtype)


def bmm_left(f, x, out_dtype, gate=None):
    p, k, c = x.shape
    mo = f.shape[0]
    tc = min(c, 2048)
    xspec = pl.BlockSpec((1, k, tc), lambda i, j: (i, 0, j))
    ospec = pl.BlockSpec((1, mo, tc), lambda i, j: (i, 0, j))
    in_specs = [pl.BlockSpec((mo, k), lambda i, j: (0, 0)), xspec]
    args = [f, x]
    if gate is not None:
        in_specs += [ospec, ospec, pl.BlockSpec((1, tc), lambda i, j: (0, j))]
        args += list(gate)
    return pl.pallas_call(
        functools.partial(_bmm_left_kernel, gated=gate is not None),
        grid=(p, c // tc),
        in_specs=in_specs, out_specs=ospec,
        out_shape=jax.ShapeDtypeStruct((p, mo, c), out_dtype),
        compiler_params=_cparams(("arbitrary", "arbitrary")),
        name="bmm_left_gated" if gate is not None else "bmm_left",
    )(*args)


def _stride_dft_kernel(*refs, gated):
    if gated:
        f_ref, x_ref, g_ref, v_ref, bias_ref, o_ref = refs
    else:
        f_ref, x_ref, o_ref = refs
    td = x_ref.shape[-1]
    x = x_ref[0].reshape(-1, td).astype(BF16)
    y = _dot(f_ref[...], x).reshape(o_ref.shape[1:])
    if gated:
        y = g_ref[0].astype(F32) * (y + bias_ref[...] * v_ref[0].astype(F32))
    o_ref[0] = y.astype(o_ref.dtype)


def stride_dft(fk, x, out_dtype, gate=None):
    p, k, r, d = x.shape
    sub = FFT_SUB
    mo = fk.shape[0] // sub
    td = min(d, 1024)
    xspec = pl.BlockSpec((1, k, sub, td), lambda i, j, l: (i, 0, j, l))
    ospec = pl.BlockSpec((1, mo, sub, td), lambda i, j, l: (i, 0, j, l))
    in_specs = [pl.BlockSpec(fk.shape, lambda i, j, l: (0, 0)), xspec]
    args = [fk, x]
    if gate is not None:
        in_specs += [ospec, ospec, pl.BlockSpec((1, td), lambda i, j, l: (0, l))]
        args += list(gate)
    return pl.pallas_call(
        functools.partial(_stride_dft_kernel, gated=gate is not None),
        grid=(p, r // sub, d // td),
        in_specs=in_specs, out_specs=ospec,
        out_shape=jax.ShapeDtypeStruct((p, mo, r, d), out_dtype),
        compiler_params=_cparams(("arbitrary", "arbitrary", "arbitrary")),
        name="stride_dft_gated" if gate is not None else "stride_dft",
    )(*args)


def _slab_fwd_kernel(m1_ref, t_ref, o_ref):
    r = FFT_SLAB
    t = t_ref[0, :, 0].reshape(2 * r, t_ref.shape[-1]).astype(BF16)
    x = _dot(m1_ref[0], t)
    o_ref[:, 0] = x.reshape(2, r, x.shape[-1])


def slab_spectrum(m1, t):
    _, a, r, d = t.shape
    td = min(d, 2048)
    return pl.pallas_call(
        _slab_fwd_kernel, grid=(a, d // td),
        in_specs=[pl.BlockSpec((1, 2 * r, 2 * r), lambda i, j: (i, 0, 0)),
                  pl.BlockSpec((1, 2, 1, r, td), lambda i, j: (0, 0, i, 0, j))],
        out_specs=pl.BlockSpec((2, 1, r, td), lambda i, j: (0, i, 0, j)),
        out_shape=jax.ShapeDtypeStruct((2, a, r, d), F32),
        compiler_params=_cparams(("arbitrary", "arbitrary")),
        name="slab_spectrum",
    )(m1, t.reshape(1, 2, a, r, d))


def _slab_conv_kernel(m1_ref, m2_ref, h_ref, t_ref, o_ref):
    r = FFT_SLAB
    td = t_ref.shape[-1]
    t = t_ref[0, :, 0].reshape(2 * r, td)
    x = _dot(m1_ref[0], t)
    xr, xi = x[:r], x[r:]
    hr, hi = h_ref[0, 0], h_ref[1, 0]
    y = jnp.concatenate([xr * hr - xi * hi, xr * hi + xi * hr], axis=0).astype(BF16)
    u = _dot(m2_ref[0], y)
    o_ref[0, :, 0] = u.reshape(2, r, td).astype(o_ref.dtype)


def slab_conv(m1, m2, h, t):
    p, _, a, r, d = t.shape
    td = min(d, 2048)
    mspec = pl.BlockSpec((1, 2 * r, 2 * r), lambda i, j, k: (i, 0, 0))
    tspec = pl.BlockSpec((1, 2, 1, r, td), lambda i, j, k: (k, 0, i, 0, j))
    return pl.pallas_call(
        _slab_conv_kernel, grid=(a, d // td, p),
        in_specs=[mspec, mspec, pl.BlockSpec((2, 1, r, td), lambda i, j, k: (0, i, 0, j)), tspec],
        out_specs=tspec,
        out_shape=jax.ShapeDtypeStruct(t.shape, BF16),
        compiler_params=_cparams(("arbitrary", "arbitrary", "arbitrary")),
        name="slab_conv",
    )(m1, m2, h, t)


def _direct_conv_kernel(f1_ref, f2_ref, h_ref, x_ref, g_ref, bias_ref, o_ref):
    xin = x_ref[0]
    half = f1_ref.shape[0] // 2
    x = _dot(f1_ref[...], xin)
    xr, xi = x[:half], x[half:]
    hr, hi = h_ref[0], h_ref[1]
    y = jnp.concatenate([xr * hr - xi * hi, xr * hi + xi * hr], axis=0).astype(BF16)
    u = _dot(f2_ref[...], y)
    o_ref[0] = (g_ref[0].astype(F32) * (u + bias_ref[...] * xin.astype(F32))).astype(o_ref.dtype)


def direct_conv(f1, f2, h, x, g, bias_row):
    p, n2, d = x.shape
    td = 512
    xspec = pl.BlockSpec((1, n2, td), lambda i, j: (i, 0, j))
    return pl.pallas_call(
        _direct_conv_kernel, grid=(p, d // td),
        in_specs=[pl.BlockSpec(f1.shape, lambda i, j: (0, 0)),
                  pl.BlockSpec(f2.shape, lambda i, j: (0, 0)),
                  pl.BlockSpec((2, n2, td), lambda i, j: (0, 0, j)),
                  xspec, xspec, pl.BlockSpec((1, td), lambda i, j: (0, j))],
        out_specs=xspec,
        out_shape=jax.ShapeDtypeStruct((p, n2, d), BF16),
        compiler_params=_cparams(("arbitrary", "arbitrary")),
        name="direct_conv",
    )(f1, f2, h, x, g, bias_row)


def _cplx_real_form(c):
    return np.block([[c.real, -c.imag], [c.imag, c.real]])


def _dft_tables(n):
    nn = 2 * n
    r = FFT_SLAB
    if nn <= 4 * r:
        k = np.arange(nn)[:, None]
        m = np.arange(nn)[None, :]
        fc = np.exp(-2j * np.pi * ((k * m) % nn) / nn)
        f1 = _cplx_real_form(fc[:, :n])
        f2 = _cplx_real_form(np.conj(fc.T)[:n, :] / nn)
        fh = np.concatenate([fc.real, fc.imag], axis=0)
        return dict(f1=jnp.asarray(f1, BF16), f2=jnp.asarray(f2, BF16), fh=jnp.asarray(fh, BF16))
    a = nn // r
    ka = np.arange(a)[:, None]
    aa = np.arange(a)[None, :]
    fa = np.exp(-2j * np.pi * ((ka * aa) % a) / a)
    fa_fwd = _cplx_real_form(fa[:, :a // 2])
    fa_inv = _cplx_real_form(np.conj(fa.T)[:a // 2, :] / nn)
    fa_h = np.concatenate([fa.real, fa.imag], axis=0)
    kb = np.arange(r)[None, :, None]
    bb = np.arange(r)[None, None, :]
    kk = np.arange(a)[:, None, None]
    m1c = np.exp(-2j * np.pi * ((bb * (kk + a * kb)) % nn) / nn)
    m1 = np.stack([_cplx_real_form(m1c[i]) for i in range(a)])
    m2 = np.stack([_cplx_real_form(np.conj(m1c[i].T)) for i in range(a)])
    eye = np.eye(FFT_SUB)
    kron = lambda f: jnp.asarray(np.kron(f, eye), BF16)
    return dict(fa_fwd=kron(fa_fwd), fa_inv=kron(fa_inv), fa_h=kron(fa_h),
                m1=jnp.asarray(m1, BF16), m2=jnp.asarray(m2, BF16))


def hyena_long_convs(x1, x2, v, hfull, f_bias):
    b, n, d = v.shape
    nn = 2 * n
    r = FFT_SLAB
    tabs = _dft_tables(n)
    p = b // 2
    bias = f_bias.astype(F32)
    if 'f1' in tabs:
        h = bmm_left(tabs['fh'], hfull, F32).reshape(HYENA_ORDER, 2, nn, d)
        pair = lambda t: t.reshape(p, nn, d)
        z = direct_conv(tabs['f1'], tabs['f2'], h[0], pair(v), pair(x1), bias[0].reshape(1, d))
        y = direct_conv(tabs['f1'], tabs['f2'], h[1], z, pair(x2), bias[1].reshape(1, d))
        return y.reshape(b, n, d)
    a = nn // r
    th = stride_dft(tabs['fa_h'], hfull.reshape(HYENA_ORDER, a, r, d), BF16)
    h = [slab_spectrum(tabs['m1'], th[o].reshape(2, a, r, d)) for o in range(HYENA_ORDER)]
    pair = lambda t: t.reshape(p, a, r, d)

    def conv(sig, gate, o):
        t = stride_dft(tabs['fa_fwd'], pair(sig), BF16)
        u = slab_conv(tabs['m1'], tabs['m2'], h[o], t.reshape(p, 2, a, r, d))
        return stride_dft(tabs['fa_inv'], u.reshape(p, 2 * a, r, d), BF16,
                          gate=(pair(gate), pair(sig), bias[o].reshape(1, d)))

    z = conv(v, x1, 0)
    y = conv(z.reshape(b, n, d), x2, 1)
    return y.reshape(b, n, d)


def hyena_mixer(u_lat, u_ctx, w_in, conv_w, conv_b, f_w1, f_b1, f_w2, f_b2, f_w3, f_b3, f_w4,
                f_freq, f_bias, ctx_out):
    d = D_MODEL
    w = w_in.astype(BF16)

    def run(u):
        n = u.shape[1]
        hfull = hyena_filters(n, f_w1, f_b1, f_w2, f_b2, f_w3, f_b3, f_w4, f_freq)
        xin = matmul(u, w, BF16, 512)
        x1, x2, v = [dwconv(xin, conv_w[:, i * d:(i + 1) * d], conv_b[i * d:(i + 1) * d], i * d, d, act=False)
                     for i in range(3)]
        return hyena_long_convs(x1, x2, v, hfull, f_bias)

    return run(u_lat), (run(u_ctx) if ctx_out else None)


def kernel(x, c, ctx, c_ctx, ada_w, ada_b, ln_g, ln_b, mlp_w1, mlp_w2, ssd_w_in, ssd_conv_w, ssd_conv_b, ssd_dt_bias, ssd_a_log, ssd_d, ssd_norm_g, ssd_w_out, attn_w_qkv, attn_sink, attn_w_o, hy_w_in, hy_conv_w, hy_conv_b, hy_f_w1, hy_f_b1, hy_f_w2, hy_f_b2, hy_f_w3, hy_f_b3, hy_f_w4, hy_f_freq, hy_f_bias, hy_w_out):
    bsz, _, d = x.shape
    depth = ada_w.shape[0]
    assert bsz + 1 <= COND_ROWS and bsz % 2 == 0
    cond = jnp.zeros((COND_ROWS, d), F32).at[:bsz].set(c.astype(F32)).at[bsz].set(c_ctx.astype(F32))
    mods = ada_mods(cond, ada_w.astype(F32), ada_b.astype(F32))

    def mod_l(i, k):
        return mods[i, :bsz, k * d:(k + 1) * d][:, None, :]

    def mod_c(i, k):
        return jnp.broadcast_to(mods[i, bsz, k * d:(k + 1) * d][None, None, :], (bsz, 1, d))

    zero = jnp.zeros((bsz, 1, d), F32)
    xl, xc = x.astype(F32), ctx.astype(F32)
    ul = modulate(xl, mod_l(0, 1), mod_l(0, 0))
    uc = modulate(xc, mod_c(0, 1), mod_c(0, 0))
    for i in range(depth):
        last = i == depth - 1
        kind = MIXER_OF_LAYER[i]
        j = MIXER_OF_LAYER[:i].count(kind)
        if kind == 0:
            yl, yc = ssd_mixer(ul, uc, ssd_w_in[j], ssd_conv_w[j].astype(F32), ssd_conv_b[j].astype(F32),
                               ssd_dt_bias[j], ssd_a_log[j], ssd_d[j], ssd_norm_g[j], not last)
            w_out = ssd_w_out[j]
        elif kind == 1:
            yl, yc = attn_mixer(ul, uc, attn_w_qkv[j], attn_sink[j], not last)
            w_out = attn_w_o[j]
        else:
            yl, yc = hyena_mixer(ul, uc, hy_w_in[j], hy_conv_w[j].astype(F32), hy_conv_b[j].astype(F32),
                                 hy_f_w1[j], hy_f_b1[j], hy_f_w2[j], hy_f_b2[j], hy_f_w3[j], hy_f_b3[j],
                                 hy_f_w4[j], hy_f_freq[j], hy_f_bias[j], not last)
            w_out = hy_w_out[j]
        w_out = w_out.astype(BF16)
        w1 = mlp_w1[i].astype(BF16)
        w2 = mlp_w2[i].astype(BF16)
        nsc_l, nsh_l = (zero, zero) if last else (mod_l(i + 1, 1), mod_l(i + 1, 0))
        xl, ul = matmul_res_ln(yl, w_out, xl, mod_l(i, 2), ln_g[i, 0], ln_b[i, 0], mod_l(i, 4), mod_l(i, 3))
        xl, ul = mlp_res_ln(ul, w1, w2, xl, mod_l(i, 5), ln_g[i, 1], ln_b[i, 1], nsc_l, nsh_l)
        if not last:
            xc, uc = matmul_res_ln(yc, w_out, xc, mod_c(i, 2), ln_g[i, 0], ln_b[i, 0], mod_c(i, 4), mod_c(i, 3))
            xc, uc = mlp_res_ln(uc, w1, w2, xc, mod_c(i, 5), ln_g[i, 1], ln_b[i, 1],
                                mod_c(i + 1, 1), mod_c(i + 1, 0))
    return xl.astype(x.dtype)
```

```python
import functools
import math

import numpy as np
import jax
import jax.numpy as jnp
from jax import lax
from jax.experimental import pallas as pl
from jax.experimental.pallas import tpu as pltpu

F32 = jnp.float32
BF16 = jnp.bfloat16
HIGHEST = lax.Precision.HIGHEST

D_MODEL = 2048
DEPTH = 4
GRID_W = 64
N_MIXERS = 3
MIXER_OF_LAYER = tuple(i % N_MIXERS for i in range(DEPTH))
ALPHA = (2.0 * DEPTH) ** 0.25
LN_EPS = 1e-5
RMS_EPS = 1e-5
N_MOD = 6
MLP_HIDDEN = 4 * D_MODEL

D_INNER = 2 * D_MODEL
SSM_HEAD_DIM = 64
SSM_HEADS = D_INNER // SSM_HEAD_DIM
SSM_GROUPS = 8
SSM_HPG = SSM_HEADS // SSM_GROUPS
SSM_STATE = 128
SSM_CONV_W = 5
SSM_CHUNK = 128
SSM_XBC = D_INNER + 2 * SSM_GROUPS * SSM_STATE
SSM_GW = D_INNER // SSM_GROUPS
SSM_SPLIT_ROWS = 6 * SSM_HPG
SSM_GPS = 8
LOG2E = math.log2(math.e)

ATTN_HEAD_DIM = 64
ATTN_Q_HEADS = D_MODEL // ATTN_HEAD_DIM
ATTN_KV_HEADS = 4
ATTN_GROUP = ATTN_Q_HEADS // ATTN_KV_HEADS
ATTN_Q_W = ATTN_Q_HEADS * ATTN_HEAD_DIM
ATTN_KV_W = ATTN_KV_HEADS * ATTN_HEAD_DIM
WINDOW = 128
ATTN_BLOCK = 128
ROPE_BASE = 10000.0
ATTN_MASKED = -1e30

HYENA_ORDER = 2
HYENA_SHORT_W = 3
HYENA_EMB = 33
HYENA_FILTER_W = 64
HYENA_DECAY_FAST = 0.3
HYENA_DECAY_SLOW = 1.5
HYENA_DECAY_TARGET = 1e-2

LANE = 128
FFT_SLAB = 128
FFT_SUB = 16
COND_ROWS = 16
VMEM_LIMIT_MB = 56


def _cparams(sem, vmem_mb=VMEM_LIMIT_MB):
    return pltpu.CompilerParams(dimension_semantics=sem,
                                vmem_limit_bytes=vmem_mb * 1024 * 1024)


def _dot(a, b, precision=None):
    return jnp.dot(a, b, preferred_element_type=F32, precision=precision)


def _dot_nt(a, b):
    return lax.dot_general(a, b, (((1,), (1,)), ((), ())), preferred_element_type=F32)


def _dot_tn(a, b):
    return lax.dot_general(a, b, (((0,), (0,)), ((), ())), preferred_element_type=F32)


def _silu(x):
    return x * jax.nn.sigmoid(x)


def _split_parts(v, parts):
    out = []
    for _ in range(parts):
        piece = v.astype(BF16).astype(F32)
        out.append(piece)
        v = v - piece
    return out


def _ada_kernel(c_ref, w_ref, b_ref, o_ref):
    s = _silu(c_ref[...])
    o_ref[0] = _dot(s, w_ref[0], HIGHEST) + b_ref[0]


def ada_mods(cond, ada_w, ada_b):
    depth, d, n6 = ada_w.shape
    tn = 512
    return pl.pallas_call(
        _ada_kernel,
        grid=(depth, n6 // tn),
        in_specs=[pl.BlockSpec((COND_ROWS, d), lambda l, j: (0, 0)),
                  pl.BlockSpec((1, d, tn), lambda l, j: (l, 0, j)),
                  pl.BlockSpec((1, 1, tn), lambda l, j: (l, 0, j))],
        out_specs=pl.BlockSpec((1, COND_ROWS, tn), lambda l, j: (l, 0, j)),
        out_shape=jax.ShapeDtypeStruct((depth, COND_ROWS, n6), F32),
        compiler_params=_cparams(("arbitrary", "arbitrary")),
        name="ada_mods",
    )(cond, ada_w, ada_b.reshape(depth, 1, n6))


def _modulate_kernel(x_ref, sc_ref, sh_ref, o_ref):
    o_ref[0] = (x_ref[0] * (1.0 + sc_ref[0]) + sh_ref[0]).astype(o_ref.dtype)


def modulate(x, sc, sh):
    b, n, d = x.shape
    tm = min(n, 1024)
    row = pl.BlockSpec((1, tm, d), lambda i, j: (i, j, 0))
    vec = pl.BlockSpec((1, 1, d), lambda i, j: (i, 0, 0))
    return pl.pallas_call(
        _modulate_kernel, grid=(b, n // tm),
        in_specs=[row, vec, vec], out_specs=row,
        out_shape=jax.ShapeDtypeStruct((b, n, d), BF16),
        compiler_params=_cparams(("arbitrary", "arbitrary")),
        name="modulate",
    )(x, sc, sh)


def _mm_kernel(a_ref, w_ref, o_ref):
    o_ref[0] = _dot(a_ref[0], w_ref[...]).astype(o_ref.dtype)


def matmul(a, w, out_dtype, tn):
    b, n, k = a.shape
    nn = w.shape[1]
    tm = min(n, 1024)
    tn = min(tn, nn)
    return pl.pallas_call(
        _mm_kernel, grid=(b, n // tm, nn // tn),
        in_specs=[pl.BlockSpec((1, tm, k), lambda i, j, l: (i, j, 0)),
                  pl.BlockSpec((k, tn), lambda i, j, l: (0, l))],
        out_specs=pl.BlockSpec((1, tm, tn), lambda i, j, l: (i, j, l)),
        out_shape=jax.ShapeDtypeStruct((b, n, nn), out_dtype),
        compiler_params=_cparams(("arbitrary", "arbitrary", "arbitrary")),
        name="matmul",
    )(a, w)


def _res_ln(x, y, gate, g, b):
    h = ALPHA * x + gate * y
    mu = jnp.mean(h, axis=-1, keepdims=True)
    hc = h - mu
    var = jnp.mean(hc * hc, axis=-1, keepdims=True)
    return hc * lax.rsqrt(var + LN_EPS) * g + b


def _mm_res_ln_kernel(a_ref, w_ref, x_ref, gate_ref, g_ref, b_ref, sc_ref, sh_ref, xo_ref, uo_ref):
    xn = _res_ln(x_ref[0], _dot(a_ref[0], w_ref[...]), gate_ref[0], g_ref[...], b_ref[...])
    xo_ref[0] = xn
    uo_ref[0] = (xn * (1.0 + sc_ref[0]) + sh_ref[0]).astype(uo_ref.dtype)


def matmul_res_ln(a, w, x, gate, ln_g, ln_b, sc_next, sh_next):
    b, n, kk = a.shape
    d = w.shape[1]
    tm = min(n, 512)
    row = pl.BlockSpec((1, tm, d), lambda i, j: (i, j, 0))
    vec = pl.BlockSpec((1, 1, d), lambda i, j: (i, 0, 0))
    par = pl.BlockSpec((1, d), lambda i, j: (0, 0))
    return pl.pallas_call(
        _mm_res_ln_kernel, grid=(b, n // tm),
        in_specs=[pl.BlockSpec((1, tm, kk), lambda i, j: (i, j, 0)),
                  pl.BlockSpec((kk, d), lambda i, j: (0, 0), pipeline_mode=pl.Buffered(1)),
                  row, vec, par, par, vec, vec],
        out_specs=[row, row],
        out_shape=[jax.ShapeDtypeStruct((b, n, d), F32), jax.ShapeDtypeStruct((b, n, d), BF16)],
        compiler_params=_cparams(("arbitrary", "arbitrary")),
        name="matmul_res_ln",
    )(a, w, x, gate, ln_g.reshape(1, d), ln_b.reshape(1, d), sc_next, sh_next)


def _mlp_kernel(u_ref, w1_ref, w2_ref, x_ref, gate_ref, g_ref, b_ref, sc_ref, sh_ref,
                xo_ref, uo_ref, acc_ref):
    k = pl.program_id(2)

    @pl.when(k == 0)
    def _():
        acc_ref[...] = jnp.zeros_like(acc_ref)

    h = jnp.maximum(_dot(u_ref[0], w1_ref[...]), 0.0)
    acc_ref[...] += _dot((h * h).astype(BF16), w2_ref[...])

    @pl.when(k == pl.num_programs(2) - 1)
    def _():
        xn = _res_ln(x_ref[0], acc_ref[...], gate_ref[0], g_ref[...], b_ref[...])
        xo_ref[0] = xn
        uo_ref[0] = (xn * (1.0 + sc_ref[0]) + sh_ref[0]).astype(uo_ref.dtype)


def mlp_res_ln(u, w1, w2, x, gate, ln_g, ln_b, sc_next, sh_next):
    b, n, d = u.shape
    hid = w1.shape[1]
    tm = min(n, 512)
    th = 1024
    row = pl.BlockSpec((1, tm, d), lambda i, j, k: (i, j, 0))
    vec = pl.BlockSpec((1, 1, d), lambda i, j, k: (i, 0, 0))
    par = pl.BlockSpec((1, d), lambda i, j, k: (0, 0))
    return pl.pallas_call(
        _mlp_kernel, grid=(b, n // tm, hid // th),
        in_specs=[row,
                  pl.BlockSpec((d, th), lambda i, j, k: (0, k)),
                  pl.BlockSpec((th, d), lambda i, j, k: (k, 0)),
                  row, vec, par, par, vec, vec],
        out_specs=[row, row],
        out_shape=[jax.ShapeDtypeStruct((b, n, d), F32), jax.ShapeDtypeStruct((b, n, d), BF16)],
        scratch_shapes=[pltpu.VMEM((tm, d), F32)],
        compiler_params=_cparams(("arbitrary", "arbitrary", "arbitrary")),
        name="mlp_res_ln",
    )(u, w1, w2, x, gate, ln_g.reshape(1, d), ln_b.reshape(1, d), sc_next, sh_next)


def _dwconv_kernel(x_ref, w_ref, b_ref, o_ref, *, width, act):
    x = x_ref[0].astype(F32)
    n = x.shape[0]
    half = width // 2
    rows = lax.broadcasted_iota(jnp.int32, x.shape, 0)
    acc = x * w_ref[half:half + 1, :] + b_ref[...]
    for k in range(width):
        off = k - half
        if off == 0:
            continue
        shifted = pltpu.roll(x, (-off) % n, axis=0)
        valid = (rows >= -off) if off < 0 else (rows < n - off)
        acc = acc + jnp.where(valid, shifted, 0.0) * w_ref[k:k + 1, :]
    if act:
        acc = _silu(acc)
    o_ref[0] = acc.astype(o_ref.dtype)


def dwconv(x, w, bias, col0, ncols, act, out_dtype=BF16):
    b, n, _ = x.shape
    width = w.shape[0]
    tc = 256 if n > 1024 else 512
    tc = min(tc, ncols)
    cb0 = col0 // tc
    return pl.pallas_call(
        functools.partial(_dwconv_kernel, width=width, act=act),
        grid=(b, ncols // tc),
        in_specs=[pl.BlockSpec((1, n, tc), lambda i, j: (i, 0, cb0 + j)),
                  pl.BlockSpec((width, tc), lambda i, j: (0, j)),
                  pl.BlockSpec((1, tc), lambda i, j: (0, j))],
        out_specs=pl.BlockSpec((1, n, tc), lambda i, j: (i, 0, j)),
        out_shape=jax.ShapeDtypeStruct((b, n, ncols), out_dtype),
        compiler_params=_cparams(("arbitrary", "arbitrary")),
        name="dwconv",
    )(x, w, bias.reshape(1, ncols))


def _dtprep_kernel(raw_ref, bias_ref, a_ref, at_ref, r0_ref, ft_ref):
    x = raw_ref[0] + bias_ref[...]
    dt = jnp.maximum(x, 0.0) + jnp.log1p(jnp.exp(-jnp.abs(x)))
    adt = dt * a_ref[...]
    q = x.shape[0]
    r = lax.broadcasted_iota(jnp.int32, (q, q), 0)
    c = lax.broadcasted_iota(jnp.int32, (q, q), 1)
    fwd = _dot((c <= r).astype(F32), adt, HIGHEST)
    bwd = _dot((c >= r).astype(F32), adt, HIGHEST)
    lane = lax.broadcasted_iota(jnp.int32, x.shape, 1)
    cum = jnp.where(lane < SSM_HEADS, fwd, bwd)
    tot = jnp.where(lane[0:1] < SSM_HEADS, cum[q - 1:q], cum[0:1])
    c3 = _split_parts((cum * LOG2E).T, 3)
    fparts = [s for v in (dt, jnp.exp(cum), jnp.exp(tot - cum) * dt) for s in _split_parts(v.T, 2)]
    ones = jnp.ones((3 * SSM_HPG, q), F32)
    for d in range(2):
        for g in range(SSM_GROUPS):
            rows = slice(d * SSM_HEADS + g * SSM_HPG, d * SSM_HEADS + (g + 1) * SSM_HPG)
            at_ref[0, 0, d, g] = jnp.concatenate([s[rows] for s in c3] + [ones], axis=0).astype(BF16)
            r0_ref[0, 0, d, g] = jnp.concatenate([ones] + [-s[rows] for s in c3], axis=0).astype(BF16)
            ft_ref[0, 0, d, g] = jnp.concatenate([s[rows] for s in fparts], axis=0).astype(BF16)


def ssd_dtprep(dt_raw, dt_bias, a_log):
    b, n, w = dt_raw.shape
    q = SSM_CHUNK
    nc = n // q
    a = -jnp.exp(a_log.astype(F32)).reshape(1, w)
    par = pl.BlockSpec((1, w), lambda i, j: (0, 0))
    oshape = (b, nc, 2, SSM_GROUPS, SSM_SPLIT_ROWS, q)
    ospec = pl.BlockSpec((1, 1) + oshape[2:], lambda i, j: (i, j, 0, 0, 0, 0))
    return pl.pallas_call(
        _dtprep_kernel, grid=(b, nc),
        in_specs=[pl.BlockSpec((1, q, w), lambda i, j: (i, j, 0)), par, par],
        out_specs=[ospec] * 3,
        out_shape=[jax.ShapeDtypeStruct(oshape, BF16)] * 3,
        compiler_params=_cparams(("arbitrary", "arbitrary")),
        name="ssd_dtprep",
    )(dt_raw, dt_bias.reshape(1, w).astype(F32), a)


def _ssd_scan_kernel(*refs, reverse, finish):
    if finish:
        (x_ref, b_ref, c_ref, at_ref, r0_ref, ft_ref, bmask_ref, eexp_ref, h0_ref,
         yf_ref, z_ref, dsk_ref, ng_ref, y_ref, hout_ref, h_scr) = refs
    else:
        (x_ref, b_ref, c_ref, at_ref, r0_ref, ft_ref, bmask_ref, eexp_ref, h0_ref,
         y_ref, hout_ref, h_scr) = refs
    step = pl.program_id(2)

    @pl.when(step == 0)
    def _():
        h_scr[...] = h0_ref[0]

    q = SSM_CHUNK
    p = SSM_HEAD_DIM
    gw = SSM_GW
    ns = SSM_STATE
    ri = lax.broadcasted_iota(jnp.int32, (q, q), 0)
    ci = lax.broadcasted_iota(jnp.int32, (q, q), 1)
    keep = (ci >= ri) if reverse else (ci <= ri)
    left = lax.broadcasted_iota(jnp.int32, (q, 2 * p), 1) < p

    for gi in range(SSM_GPS):
        gcols = slice(gi * gw, (gi + 1) * gw)
        x = x_ref[0, :, gcols].astype(F32)
        bm = b_ref[0, :, gi * ns:(gi + 1) * ns]
        cm = c_ref[0, :, gi * ns:(gi + 1) * ns]
        cb = jnp.where(keep, _dot_nt(cm, bm), 0.0)
        rhs = jnp.concatenate([r0_ref[0, 0, 0, gi]] * SSM_HPG, axis=1) * bmask_ref[...]
        diff = _dot_tn(at_ref[0, 0, 0, gi], rhs)
        ft = ft_ref[0, 0, 0, gi]
        fac = _dot_tn(ft, eexp_ref[...])
        ec = fac[:, :gw]
        xw = (x * fac[:, gw:]).astype(BF16)
        etot = ec[0:1] if reverse else ec[q - 1:q]
        ht = h_scr[gi]
        yoff = _dot(cm, ht.astype(BF16)) * ec
        dtj = ft[0:SSM_HPG].astype(F32) + ft[SSM_HPG:2 * SSM_HPG].astype(F32)

        ys = []
        for pr in range(SSM_HPG // 2):
            cols = slice(pr * 2 * p, (pr + 1) * 2 * p)
            xp = x[:, cols]
            yp = yoff[:, cols]
            for hd, xm in ((2 * pr, jnp.where(left, xp, 0.0)), (2 * pr + 1, jnp.where(left, 0.0, xp))):
                dec = jnp.exp2(jnp.minimum(diff[:, hd * q:(hd + 1) * q], 0.0))
                yp = yp + _dot((cb * dtj[hd:hd + 1] * dec).astype(BF16), xm.astype(BF16))
            ys.append(yp)
        y = jnp.concatenate(ys, axis=1)
        h_scr[gi] = ht * etot + _dot_tn(bm, xw)

        if finish:
            yt = y + yf_ref[0, :, gcols] + dsk_ref[:, gcols] * x
            yt = yt * _silu(z_ref[0, :, gcols].astype(F32))
            ms = jnp.mean(yt * yt, axis=-1, keepdims=True)
            y_ref[0, :, gcols] = (yt * lax.rsqrt(ms + RMS_EPS) * ng_ref[:, gcols]).astype(y_ref.dtype)
        else:
            y_ref[0, :, gcols] = y

    @pl.when(step == pl.num_programs(2) - 1)
    def _():
        hout_ref[0] = h_scr[...]


def _ssd_tables():
    q, hg, p = SSM_CHUNK, SSM_HPG, SSM_HEAD_DIM
    r = np.arange(SSM_SPLIT_ROWS)[:, None]
    bmask = (r % hg == np.arange(hg * q)[None, :] // q)
    c = np.arange(2 * SSM_GW)[None, :]
    eexp = (r // (2 * hg) == 1 + c // SSM_GW) & (r % hg == (c % SSM_GW) // p)
    return jnp.asarray(bmask, BF16), jnp.asarray(eexp, BF16)


def ssd_scan(xbc, at, r0, ft, h0, *, reverse, fin=None):
    b, n, _ = xbc.shape
    q, g, gw, ns = SSM_CHUNK, SSM_GROUPS, SSM_GW, SSM_STATE
    sr = SSM_SPLIT_ROWS
    nc = n // q
    bmask, eexp = _ssd_tables()

    def ch(k):
        return nc - 1 - k if reverse else k

    gps = SSM_GPS
    bcol = D_INNER // (gps * ns)
    direction = 1 if reverse else 0
    opspec = pl.BlockSpec((1, 1, 1, gps, sr, q), lambda i, j, k: (i, ch(k), direction, j, 0, 0))
    in_specs = [
        pl.BlockSpec((1, q, gps * gw), lambda i, j, k: (i, ch(k), j)),
        pl.BlockSpec((1, q, gps * ns), lambda i, j, k: (i, ch(k), bcol + j)),
        pl.BlockSpec((1, q, gps * ns), lambda i, j, k: (i, ch(k), bcol + g // gps + j)),
        opspec, opspec, opspec,
        pl.BlockSpec(bmask.shape, lambda i, j, k: (0, 0)),
        pl.BlockSpec(eexp.shape, lambda i, j, k: (0, 0)),
        pl.BlockSpec((1, gps, ns, gw), lambda i, j, k: (i, j, 0, 0)),
    ]
    args = [xbc, xbc, xbc, at, r0, ft, bmask, eexp, h0]
    yspec = pl.BlockSpec((1, q, gps * gw), lambda i, j, k: (i, ch(k), j))
    if fin is not None:
        y_other, z, dsk, ng = fin
        in_specs += [yspec, yspec,
                     pl.BlockSpec((1, gps * gw), lambda i, j, k: (0, j)),
                     pl.BlockSpec((1, gps * gw), lambda i, j, k: (0, j))]
        args += [y_other, z, dsk, ng]
    return pl.pallas_call(
        functools.partial(_ssd_scan_kernel, reverse=reverse, finish=fin is not None),
        grid=(b, g // gps, nc),
        in_specs=in_specs,
        out_specs=[yspec, pl.BlockSpec((1, gps, ns, gw), lambda i, j, k: (i, j, 0, 0))],
        out_shape=[jax.ShapeDtypeStruct((b, n, D_INNER), BF16 if fin is not None else F32),
                   jax.ShapeDtypeStruct((b, g, ns, gw), F32)],
        scratch_shapes=[pltpu.VMEM((gps, ns, gw), F32)],
        compiler_params=_cparams(("arbitrary", "arbitrary", "arbitrary")),
        name="ssd_scan_bwd" if reverse else "ssd_scan_fwd",
    )(*args)


def ssd_mixer(u_lat, u_ctx, w_in, conv_w, conv_b, dt_bias, a_log, d_skip, norm_g, ctx_out):
    g, hg = SSM_GROUPS, SSM_HPG
    w_z = w_in[:, :D_INNER].astype(BF16)
    w_xbc = w_in[:, D_INNER:D_INNER + SSM_XBC].astype(BF16)
    w_dt = w_in[:, D_INNER + SSM_XBC:].astype(BF16)
    dsk = jnp.repeat(d_skip.astype(F32), SSM_HEAD_DIM).reshape(1, D_INNER)
    ng = norm_g.astype(F32).reshape(1, D_INNER)

    def project(u):
        b, n, _ = u.shape
        z = matmul(u, w_z, BF16, 512)
        xbc = dwconv(matmul(u, w_xbc, BF16, 512), conv_w, conv_b, 0, SSM_XBC, act=True)
        at, r0, ft = ssd_dtprep(matmul(u, w_dt, F32, 2 * SSM_HEADS), dt_bias, a_log)
        return z, xbc, at, r0, ft

    def bidir(proj, h_f, h_b, want_y):
        z, xbc, at, r0, ft = proj
        y_f, s_f = ssd_scan(xbc, at, r0, ft, h_f, reverse=False)
        fin = (y_f, z, dsk, ng) if want_y else None
        y, s_b = ssd_scan(xbc, at, r0, ft, h_b, reverse=True, fin=fin)
        return y, s_f, s_b

    h0 = jnp.zeros((u_lat.shape[0], g, SSM_STATE, SSM_GW), F32)
    yc, hc_f, hc_b = bidir(project(u_ctx), h0, h0, ctx_out)
    yl, _, _ = bidir(project(u_lat), hc_f, hc_b, True)
    return yl, (yc if ctx_out else None)


def _rope_kernel(x_ref, cos_ref, sin_ref, o_ref):
    cos = cos_ref[...]
    sin = sin_ref[...]
    half = ATTN_HEAD_DIM // 2
    first = (lax.broadcasted_iota(jnp.int32, cos.shape, 1) % ATTN_HEAD_DIM) < half
    for t in range(x_ref.shape[2] // LANE):
        cols = slice(t * LANE, (t + 1) * LANE)
        x = x_ref[0, :, cols].astype(F32)
        partner = jnp.where(first, pltpu.roll(x, LANE - half, axis=1), pltpu.roll(x, half, axis=1))
        o_ref[0, :, cols] = (x * cos + partner * sin).astype(o_ref.dtype)


def rope(qkv, cos_t, sin_t):
    b, n, _ = qkv.shape
    w = ATTN_Q_W + ATTN_KV_W
    tm = 512
    return pl.pallas_call(
        _rope_kernel, grid=(b, n // tm),
        in_specs=[pl.BlockSpec((1, tm, w), lambda i, j: (i, j, 0)),
                  pl.BlockSpec((tm, LANE), lambda i, j: (j, 0)),
                  pl.BlockSpec((tm, LANE), lambda i, j: (j, 0))],
        out_specs=pl.BlockSpec((1, tm, w), lambda i, j: (i, j, 0)),
        out_shape=jax.ShapeDtypeStruct((b, n, w), BF16),
        compiler_params=_cparams(("arbitrary", "arbitrary")),
        name="rope",
    )(qkv, cos_t, sin_t)


def _attn_kernel(*refs, local, seq_len):
    if local:
        (q_ref, k0_ref, k1_ref, k2_ref, v0_ref, v1_ref, v2_ref, kc_ref, vc_ref, sink_ref, band_ref,
         o_ref, qx_ref, kx_ref, vx_ref) = refs
        kv_blocks = ((k0_ref, v0_ref), (k1_ref, v1_ref), (k2_ref, v2_ref))
    else:
        q_ref, kc_ref, vc_ref, sink_ref, o_ref, qx_ref, kx_ref, vx_ref = refs
        kv_blocks = ()
    blk, dh, grp = ATTN_BLOCK, ATTN_HEAD_DIM, ATTN_GROUP
    rows = grp * blk
    nloc = len(kv_blocks) * blk
    pad = LANE - dh

    @pl.when((pl.program_id(0) == 0) & (pl.program_id(1) == 0))
    def _():
        r = lax.broadcasted_iota(jnp.int32, (rows, LANE), 0) % blk
        c = lax.broadcasted_iota(jnp.int32, (rows, LANE), 1)
        cpad = lax.broadcasted_iota(jnp.int32, (rows, pad), 1)
        for kv in range(ATTN_KV_HEADS):
            qx_ref[kv, :, 0:LANE] = jnp.where(r == c, 1.0, 0.0).astype(BF16)
            qx_ref[kv, :, LANE + dh:] = jnp.where(cpad == 0, 1.0, 0.0).astype(BF16)
            kx_ref[kv] = jnp.zeros(kx_ref.shape[1:], BF16)
            if local:
                kx_ref[kv, 0:nloc, 0:LANE] = band_ref[...]
            vx_ref[kv, :, dh:] = jnp.ones((vx_ref.shape[1], pad), BF16)

    if local:
        sj = lax.broadcasted_iota(jnp.int32, (nloc, pad), 0)
        lane = lax.broadcasted_iota(jnp.int32, (nloc, pad), 1)
        kpos = (pl.program_id(1) - 1) * blk + sj
        outside = (kpos < 0) | (kpos >= seq_len)
        edge = jnp.where((lane == 0) & outside, ATTN_MASKED, 0.0).astype(BF16)

    for kv in range(ATTN_KV_HEADS):
        hc = slice(kv * dh, (kv + 1) * dh)
        if local:
            kx_ref[kv, 0:nloc, LANE + dh:] = edge
        for g in range(grp):
            c0 = (kv * grp + g) * dh
            qx_ref[kv, g * blk:(g + 1) * blk, LANE:LANE + dh] = q_ref[0, :, c0:c0 + dh]
        for t, (k_ref, v_ref) in enumerate(kv_blocks):
            kx_ref[kv, t * blk:(t + 1) * blk, LANE:LANE + dh] = k_ref[0, :, hc]
            vx_ref[kv, t * blk:(t + 1) * blk, 0:dh] = v_ref[0, :, hc]
        kx_ref[kv, nloc:, LANE:LANE + dh] = kc_ref[0, :, hc]
        vx_ref[kv, nloc:, 0:dh] = vc_ref[0, :, hc]
    heads = range(ATTN_KV_HEADS)
    s = [_dot_nt(qx_ref[kv], kx_ref[kv]) for kv in heads]
    m = [jnp.maximum(jnp.max(s[kv], axis=-1, keepdims=True), sink_ref[kv]) for kv in heads]
    p = [jnp.exp2(s[kv] - m[kv]).astype(BF16) for kv in heads]
    oe = [_dot(p[kv], vx_ref[kv]) for kv in heads]
    for kv in heads:
        den = pltpu.roll(oe[kv], dh, axis=1) + jnp.exp2(sink_ref[kv] - m[kv])
        o = (oe[kv] / den).astype(o_ref.dtype)
        for g in range(grp):
            c0 = (kv * grp + g) * dh
            o_ref[0, :, c0:c0 + dh] = o[g * blk:(g + 1) * blk, 0:dh]


def attention(qk, qkv, qkv_ctx, sink_col, *, local):
    b, n, _ = qk.shape
    nctx = qkv_ctx.shape[1]
    blk = ATTN_BLOCK
    nblk = n // blk
    kcb = ATTN_Q_W // ATTN_KV_W
    vcb = kcb + 1
    qspec = pl.BlockSpec((1, blk, ATTN_Q_W), lambda i, j: (i, j, 0))
    ctx_k = pl.BlockSpec((1, nctx, ATTN_KV_W), lambda i, j: (i, 0, kcb))
    ctx_v = pl.BlockSpec((1, nctx, ATTN_KV_W), lambda i, j: (i, 0, vcb))
    sspec = pl.BlockSpec(sink_col.shape, lambda i, j: (0, 0, 0))
    if local:
        def win(cb, off):
            return pl.BlockSpec((1, blk, ATTN_KV_W),
                                lambda i, j: (i, jnp.clip(j + off, 0, nblk - 1), cb))
        sidx = np.arange(3 * blk)[:, None]
        ridx = np.arange(blk)[None, :]
        band = jnp.asarray(np.where(np.abs(sidx - blk - ridx) <= WINDOW, 0.0, ATTN_MASKED), BF16)
        in_specs = [qspec, win(kcb, -1), win(kcb, 0), win(kcb, 1),
                    win(vcb, -1), win(vcb, 0), win(vcb, 1), ctx_k, ctx_v, sspec,
                    pl.BlockSpec(band.shape, lambda i, j: (0, 0))]
        args = [qk, qk, qk, qk, qkv, qkv, qkv, qkv_ctx, qkv_ctx, sink_col, band]
        nkeys = 3 * blk + nctx
    else:
        in_specs = [qspec, ctx_k, ctx_v, sspec]
        args = [qk, qkv_ctx, qkv_ctx, sink_col]
        nkeys = nctx
    return pl.pallas_call(
        functools.partial(_attn_kernel, local=local, seq_len=n),
        grid=(b, nblk),
        in_specs=in_specs,
        out_specs=qspec,
        out_shape=jax.ShapeDtypeStruct((b, n, ATTN_Q_W), BF16),
        scratch_shapes=[pltpu.VMEM((ATTN_KV_HEADS, ATTN_GROUP * blk, 2 * LANE), BF16),
                        pltpu.VMEM((ATTN_KV_HEADS, nkeys, 2 * LANE), BF16),
                        pltpu.VMEM((ATTN_KV_HEADS, nkeys, LANE), BF16)],
        compiler_params=_cparams(("arbitrary", "arbitrary")),
        name="attention_local" if local else "attention_ctx",
    )(*args)


def _rope_tables(n):
    rows = n // GRID_W
    row_id = np.repeat(np.arange(rows, dtype=np.float32), GRID_W)
    col_id = np.tile(np.arange(GRID_W, dtype=np.float32), rows)
    pairs = ATTN_HEAD_DIM // 4
    inv = (np.float32(ROPE_BASE) ** (-np.arange(pairs, dtype=np.float32) / np.float32(pairs))).astype(np.float32)
    ang = np.concatenate([row_id[:, None] * inv, col_id[:, None] * inv], axis=-1).astype(np.float32)
    cos, sin = np.cos(ang.astype(np.float64)), np.sin(ang.astype(np.float64))
    reps = LANE // ATTN_HEAD_DIM
    cos_t = np.tile(np.concatenate([cos, cos], axis=-1), (1, reps))
    sin_t = np.tile(np.concatenate([-sin, sin], axis=-1), (1, reps))
    return jnp.asarray(cos_t, F32), jnp.asarray(sin_t, F32)


def attn_mixer(u_lat, u_ctx, w_qkv, sink, ctx_out):
    dh = ATTN_HEAD_DIM
    perm = np.concatenate([np.arange(0, dh, 2), np.arange(1, dh, 2)])
    nqk = ATTN_Q_HEADS + ATTN_KV_HEADS
    cols = (np.arange(nqk)[:, None] * dh + perm[None, :]).reshape(-1)
    cols = np.concatenate([cols, np.arange(nqk * dh, nqk * dh + ATTN_KV_W)])
    colscale = np.where(np.arange(cols.shape[0]) < ATTN_Q_W, dh ** -0.5 * LOG2E, 1.0).astype(np.float32)
    w = (w_qkv[:, cols] * colscale).astype(BF16)
    sink_col = jnp.repeat(sink.astype(F32).reshape(ATTN_KV_HEADS, ATTN_GROUP) * LOG2E, ATTN_BLOCK, axis=1)
    sink_col = sink_col.reshape(ATTN_KV_HEADS, ATTN_GROUP * ATTN_BLOCK, 1)

    qkv_l = matmul(u_lat, w, BF16, 512)
    qkv_c = matmul(u_ctx, w, BF16, 512)
    cos_t, sin_t = _rope_tables(u_lat.shape[1])
    qk_l = rope(qkv_l, cos_t, sin_t)
    o_l = attention(qk_l, qkv_l, qkv_c, sink_col, local=True)
    o_c = attention(qkv_c, qkv_c, qkv_c, sink_col, local=False) if ctx_out else None
    return o_l, o_c


def _filter_kernel(z_ref, t_ref, w1_ref, b1_ref, w2_ref, b2_ref, w3_ref, b3_ref, fr_ref,
                   w4f_ref, w4b_ref, dl_ref, o_ref, h_scr, *, seq_len):
    @pl.when((pl.program_id(1) == 0) & (pl.program_id(2) == 0))
    def _():
        fr = fr_ref[...]
        h = jnp.sin(fr * (_dot(z_ref[...], w1_ref[...], HIGHEST) + b1_ref[...]))
        h = jnp.sin(fr * (_dot(h, w2_ref[...], HIGHEST) + b2_ref[...]))
        h_scr[...] = jnp.sin(fr * (_dot(h, w3_ref[...], HIGHEST) + b3_ref[...]))

    h = h_scr[...]
    tr = h.shape[0]
    r = pl.program_id(0) * tr + lax.broadcasted_iota(jnp.int32, (tr, 1), 0)
    hf = _dot(h, w4f_ref[...], HIGHEST)
    hb = _dot(h, w4b_ref[...], HIGHEST)
    val = jnp.where(r < seq_len, hf, hb) * jnp.exp(-t_ref[...] * dl_ref[...])
    o_ref[0] = jnp.where(r == seq_len, 0.0, val)


def hyena_filters(n, w1, b1, w2, b2, w3, b3, w4, freq):
    d = D_MODEL
    fw = HYENA_FILTER_W
    lag = np.concatenate([np.arange(n), [0], np.arange(n - 1, 0, -1)])
    t = np.linspace(0.0, 1.0, n, dtype=np.float32)[:, None]
    bands = (HYENA_EMB - 1) // 2
    wv = (np.float32(2.0 * math.pi) * np.arange(n, dtype=np.float32)[:, None] / np.float32(n)).astype(np.float32)
    f = np.linspace(1e-4, bands - 1, bands, dtype=np.float32)[None, :]
    fwv = (f * wv).astype(np.float32).astype(np.float64)
    z = np.concatenate([t, np.cos(fwv), -np.sin(fwv)], axis=-1).astype(np.float32)
    zpad = np.zeros((2 * n, fw), np.float32)
    zpad[:, :HYENA_EMB] = z[lag]
    tfull = t[lag]
    max_decay = math.log(HYENA_DECAY_TARGET) / HYENA_DECAY_FAST
    min_decay = math.log(HYENA_DECAY_TARGET) / HYENA_DECAY_SLOW
    deltas = np.abs(np.linspace(min_decay, max_decay, d, dtype=np.float32))[None, :]
    w1p = jnp.zeros((fw, fw), F32).at[:HYENA_EMB].set(w1.astype(F32))
    tr = min(2 * n, 512)
    td = 512
    nd = d // td
    par = lambda shape: pl.BlockSpec(shape, lambda i, o, j: (0, 0))
    return pl.pallas_call(
        functools.partial(_filter_kernel, seq_len=n),
        grid=(2 * n // tr, HYENA_ORDER, nd),
        in_specs=[pl.BlockSpec((tr, fw), lambda i, o, j: (i, 0)),
                  pl.BlockSpec((tr, 1), lambda i, o, j: (i, 0)),
                  par((fw, fw)), par((1, fw)), par((fw, fw)), par((1, fw)),
                  par((fw, fw)), par((1, fw)), par((1, fw)),
                  pl.BlockSpec((fw, td), lambda i, o, j: (0, (2 * o) * nd + j)),
                  pl.BlockSpec((fw, td), lambda i, o, j: (0, (2 * o + 1) * nd + j)),
                  pl.BlockSpec((1, td), lambda i, o, j: (0, j))],
        out_specs=pl.BlockSpec((1, tr, td), lambda i, o, j: (o, i, j)),
        out_shape=jax.ShapeDtypeStruct((HYENA_ORDER, 2 * n, d), F32),
        scratch_shapes=[pltpu.VMEM((tr, fw), F32)],
        compiler_params=_cparams(("arbitrary", "arbitrary", "arbitrary")),
        name="hyena_filters",
    )(jnp.asarray(zpad), jnp.asarray(tfull), w1p, b1.astype(F32).reshape(1, fw),
      w2.astype(F32), b2.astype(F32).reshape(1, fw), w3.astype(F32), b3.astype(F32).reshape(1, fw),
      freq.astype(F32).reshape(1, fw), w4.astype(F32), w4.astype(F32), jnp.asarray(deltas))


def _bmm_left_kernel(*refs, gated):
    if gated:
        f_ref, x_ref, g_ref, v_ref, bias_ref, o_ref = refs
    else:
        f_ref, x_ref, o_ref = refs
    y = _dot(f_ref[...], x_ref[0].astype(BF16))
    if gated:
        v = v_ref[0].astype(F32)
        y = g_ref[0].astype(F32) * (y + bias_ref[...] * v)
    o_ref[0] = y.astype(o_ref.dtype)


def bmm_left(f, x, out_dtype, gate=None):
    p, k, c = x.shape
    mo = f.shape[0]
    tc = min(c, 2048)
    xspec = pl.BlockSpec((1, k, tc), lambda i, j: (i, 0, j))
    ospec = pl.BlockSpec((1, mo, tc), lambda i, j: (i, 0, j))
    in_specs = [pl.BlockSpec((mo, k), lambda i, j: (0, 0)), xspec]
    args = [f, x]
    if gate is not None:
        in_specs += [ospec, ospec, pl.BlockSpec((1, tc), lambda i, j: (0, j))]
        args += list(gate)
    return pl.pallas_call(
        functools.partial(_bmm_left_kernel, gated=gate is not None),
        grid=(p, c // tc),
        in_specs=in_specs, out_specs=ospec,
        out_shape=jax.ShapeDtypeStruct((p, mo, c), out_dtype),
        compiler_params=_cparams(("arbitrary", "arbitrary")),
        name="bmm_left_gated" if gate is not None else "bmm_left",
    )(*args)


def _stride_dft_kernel(*refs, gated):
    if gated:
        f_ref, x_ref, g_ref, v_ref, bias_ref, o_ref = refs
    else:
        f_ref, x_ref, o_ref = refs
    td = x_ref.shape[-1]
    x = x_ref[0].reshape(-1, td).astype(BF16)
    y = _dot(f_ref[...], x).reshape(o_ref.shape[1:])
    if gated:
        y = g_ref[0].astype(F32) * (y + bias_ref[...] * v_ref[0].astype(F32))
    o_ref[0] = y.astype(o_ref.dtype)


def stride_dft(fk, x, out_dtype, gate=None):
    p, k, r, d = x.shape
    sub = FFT_SUB
    mo = fk.shape[0] // sub
    td = min(d, 1024)
    xspec = pl.BlockSpec((1, k, sub, td), lambda i, j, l: (i, 0, j, l))
    ospec = pl.BlockSpec((1, mo, sub, td), lambda i, j, l: (i, 0, j, l))
    in_specs = [pl.BlockSpec(fk.shape, lambda i, j, l: (0, 0)), xspec]
    args = [fk, x]
    if gate is not None:
        in_specs += [ospec, ospec, pl.BlockSpec((1, td), lambda i, j, l: (0, l))]
        args += list(gate)
    return pl.pallas_call(
        functools.partial(_stride_dft_kernel, gated=gate is not None),
        grid=(p, r // sub, d // td),
        in_specs=in_specs, out_specs=ospec,
        out_shape=jax.ShapeDtypeStruct((p, mo, r, d), out_dtype),
        compiler_params=_cparams(("arbitrary", "arbitrary", "arbitrary")),
        name="stride_dft_gated" if gate is not None else "stride_dft",
    )(*args)


def _slab_fwd_kernel(m1t_ref, t_ref, o_ref):
    r = FFT_SLAB
    t = t_ref[0, :, 0].reshape(2 * r, t_ref.shape[-1]).astype(BF16)
    xt = _dot(t.T, m1t_ref[0])
    o_ref[0, 0] = xt[:, :r]
    o_ref[1, 0] = xt[:, r:]


def slab_spectrum(m1t, t):
    _, a, r, d = t.shape
    td = min(d, 2048)
    return pl.pallas_call(
        _slab_fwd_kernel, grid=(a, d // td),
        in_specs=[pl.BlockSpec((1, 2 * r, 2 * r), lambda i, j: (i, 0, 0)),
                  pl.BlockSpec((1, 2, 1, r, td), lambda i, j: (0, 0, i, 0, j))],
        out_specs=pl.BlockSpec((2, 1, td, r), lambda i, j: (0, i, j, 0)),
        out_shape=jax.ShapeDtypeStruct((2, a, d, r), F32),
        compiler_params=_cparams(("arbitrary", "arbitrary")),
        name="slab_spectrum",
    )(m1t, t.reshape(1, 2, a, r, d))


def _slab_conv_kernel(m1t_ref, m2t_ref, h_ref, t_ref, o_ref):
    r = FFT_SLAB
    td = t_ref.shape[-1]
    t = t_ref[0, :, 0].reshape(2 * r, td)
    xt = _dot(t.T, m1t_ref[0])
    xr, xi = xt[:, :r], xt[:, r:]
    hr, hi = h_ref[0, 0], h_ref[1, 0]
    yt = jnp.concatenate([xr * hr - xi * hi, xr * hi + xi * hr], axis=1).astype(BF16)
    ut = _dot(yt, m2t_ref[0])
    o_ref[0, :, 0] = ut.T.reshape(2, r, td).astype(o_ref.dtype)


def slab_conv(m1t, m2t, h, t):
    p, _, a, r, d = t.shape
    td = min(d, 2048)
    mspec = pl.BlockSpec((1, 2 * r, 2 * r), lambda i, j, k: (i, 0, 0))
    tspec = pl.BlockSpec((1, 2, 1, r, td), lambda i, j, k: (k, 0, i, 0, j))
    return pl.pallas_call(
        _slab_conv_kernel, grid=(a, d // td, p),
        in_specs=[mspec, mspec, pl.BlockSpec((2, 1, td, r), lambda i, j, k: (0, i, j, 0)), tspec],
        out_specs=tspec,
        out_shape=jax.ShapeDtypeStruct(t.shape, BF16),
        compiler_params=_cparams(("arbitrary", "arbitrary", "arbitrary")),
        name="slab_conv",
    )(m1t, m2t, h, t)


def _direct_conv_kernel(f1_ref, f2_ref, h_ref, x_ref, g_ref, bias_ref, o_ref):
    xin = x_ref[0]
    half = f1_ref.shape[0] // 2
    x = _dot(f1_ref[...], xin)
    xr, xi = x[:half], x[half:]
    hr, hi = h_ref[0], h_ref[1]
    y = jnp.concatenate([xr * hr - xi * hi, xr * hi + xi * hr], axis=0).astype(BF16)
    u = _dot(f2_ref[...], y)
    o_ref[0] = (g_ref[0].astype(F32) * (u + bias_ref[...] * xin.astype(F32))).astype(o_ref.dtype)


def direct_conv(f1, f2, h, x, g, bias_row):
    p, n2, d = x.shape
    td = 512
    xspec = pl.BlockSpec((1, n2, td), lambda i, j: (i, 0, j))
    return pl.pallas_call(
        _direct_conv_kernel, grid=(p, d // td),
        in_specs=[pl.BlockSpec(f1.shape, lambda i, j: (0, 0)),
                  pl.BlockSpec(f2.shape, lambda i, j: (0, 0)),
                  pl.BlockSpec((2, n2, td), lambda i, j: (0, 0, j)),
                  xspec, xspec, pl.BlockSpec((1, td), lambda i, j: (0, j))],
        out_specs=xspec,
        out_shape=jax.ShapeDtypeStruct((p, n2, d), BF16),
        compiler_params=_cparams(("arbitrary", "arbitrary")),
        name="direct_conv",
    )(f1, f2, h, x, g, bias_row)


def _cplx_real_form(c):
    return np.block([[c.real, -c.imag], [c.imag, c.real]])


def _dft_tables(n):
    nn = 2 * n
    r = FFT_SLAB
    if nn <= 4 * r:
        k = np.arange(nn)[:, None]
        m = np.arange(nn)[None, :]
        fc = np.exp(-2j * np.pi * ((k * m) % nn) / nn)
        f1 = _cplx_real_form(fc[:, :n])
        f2 = _cplx_real_form(np.conj(fc.T)[:n, :] / nn)
        fh = np.concatenate([fc.real, fc.imag], axis=0)
        return dict(f1=jnp.asarray(f1, BF16), f2=jnp.asarray(f2, BF16), fh=jnp.asarray(fh, BF16))
    a = nn // r
    ka = np.arange(a)[:, None]
    aa = np.arange(a)[None, :]
    fa = np.exp(-2j * np.pi * ((ka * aa) % a) / a)
    fa_fwd = _cplx_real_form(fa[:, :a // 2])
    fa_inv = _cplx_real_form(np.conj(fa.T)[:a // 2, :] / nn)
    fa_h = np.concatenate([fa.real, fa.imag], axis=0)
    kb = np.arange(r)[None, :, None]
    bb = np.arange(r)[None, None, :]
    kk = np.arange(a)[:, None, None]
    m1c = np.exp(-2j * np.pi * ((bb * (kk + a * kb)) % nn) / nn)
    m1 = np.stack([_cplx_real_form(m1c[i]).T for i in range(a)])
    m2 = np.stack([_cplx_real_form(np.conj(m1c[i].T)).T for i in range(a)])
    eye = np.eye(FFT_SUB)
    kron = lambda f: jnp.asarray(np.kron(f, eye), BF16)
    return dict(fa_fwd=kron(fa_fwd), fa_inv=kron(fa_inv), fa_h=kron(fa_h),
                m1=jnp.asarray(m1, BF16), m2=jnp.asarray(m2, BF16))


def hyena_long_convs(x1, x2, v, hfull, f_bias):
    b, n, d = v.shape
    nn = 2 * n
    r = FFT_SLAB
    tabs = _dft_tables(n)
    p = b // 2
    bias = f_bias.astype(F32)
    if 'f1' in tabs:
        h = bmm_left(tabs['fh'], hfull, F32).reshape(HYENA_ORDER, 2, nn, d)
        pair = lambda t: t.reshape(p, nn, d)
        z = direct_conv(tabs['f1'], tabs['f2'], h[0], pair(v), pair(x1), bias[0].reshape(1, d))
        y = direct_conv(tabs['f1'], tabs['f2'], h[1], z, pair(x2), bias[1].reshape(1, d))
        return y.reshape(b, n, d)
    a = nn // r
    th = stride_dft(tabs['fa_h'], hfull.reshape(HYENA_ORDER, a, r, d), BF16)
    h = [slab_spectrum(tabs['m1'], th[o].reshape(2, a, r, d)) for o in range(HYENA_ORDER)]
    pair = lambda t: t.reshape(p, a, r, d)

    def conv(sig, gate, o):
        t = stride_dft(tabs['fa_fwd'], pair(sig), BF16)
        u = slab_conv(tabs['m1'], tabs['m2'], h[o], t.reshape(p, 2, a, r, d))
        return stride_dft(tabs['fa_inv'], u.reshape(p, 2 * a, r, d), BF16,
                          gate=(pair(gate), pair(sig), bias[o].reshape(1, d)))

    z = conv(v, x1, 0)
    y = conv(z.reshape(b, n, d), x2, 1)
    return y.reshape(b, n, d)


def hyena_mixer(u_lat, u_ctx, w_in, conv_w, conv_b, f_w1, f_b1, f_w2, f_b2, f_w3, f_b3, f_w4,
                f_freq, f_bias, ctx_out):
    d = D_MODEL
    w = w_in.astype(BF16)

    def run(u):
        n = u.shape[1]
        hfull = hyena_filters(n, f_w1, f_b1, f_w2, f_b2, f_w3, f_b3, f_w4, f_freq)
        xin = matmul(u, w, BF16, 512)
        x1, x2, v = [dwconv(xin, conv_w[:, i * d:(i + 1) * d], conv_b[i * d:(i + 1) * d], i * d, d, act=False)
                     for i in range(3)]
        return hyena_long_convs(x1, x2, v, hfull, f_bias)

    return run(u_lat), (run(u_ctx) if ctx_out else None)


def kernel(x, c, ctx, c_ctx, ada_w, ada_b, ln_g, ln_b, mlp_w1, mlp_w2, ssd_w_in, ssd_conv_w, ssd_conv_b, ssd_dt_bias, ssd_a_log, ssd_d, ssd_norm_g, ssd_w_out, attn_w_qkv, attn_sink, attn_w_o, hy_w_in, hy_conv_w, hy_conv_b, hy_f_w1, hy_f_b1, hy_f_w2, hy_f_b2, hy_f_w3, hy_f_b3, hy_f_w4, hy_f_freq, hy_f_bias, hy_w_out):
    bsz, _, d = x.shape
    depth = ada_w.shape[0]
    assert bsz + 1 <= COND_ROWS and bsz % 2 == 0
    cond = jnp.zeros((COND_ROWS, d), F32).at[:bsz].set(c.astype(F32)).at[bsz].set(c_ctx.astype(F32))
    mods = ada_mods(cond, ada_w.astype(F32), ada_b.astype(F32))

    def mod_l(i, k):
        return mods[i, :bsz, k * d:(k + 1) * d][:, None, :]

    def mod_c(i, k):
        return jnp.broadcast_to(mods[i, bsz, k * d:(k + 1) * d][None, None, :], (bsz, 1, d))

    zero = jnp.zeros((bsz, 1, d), F32)
    xl, xc = x.astype(F32), ctx.astype(F32)
    ul = modulate(xl, mod_l(0, 1), mod_l(0, 0))
    uc = modulate(xc, mod_c(0, 1), mod_c(0, 0))
    for i in range(depth):
        last = i == depth - 1
        kind = MIXER_OF_LAYER[i]
        j = MIXER_OF_LAYER[:i].count(kind)
        if kind == 0:
            yl, yc = ssd_mixer(ul, uc, ssd_w_in[j], ssd_conv_w[j].astype(F32), ssd_conv_b[j].astype(F32),
                               ssd_dt_bias[j], ssd_a_log[j], ssd_d[j], ssd_norm_g[j], not last)
            w_out = ssd_w_out[j]
        elif kind == 1:
            yl, yc = attn_mixer(ul, uc, attn_w_qkv[j], attn_sink[j], not last)
            w_out = attn_w_o[j]
        else:
            yl, yc = hyena_mixer(ul, uc, hy_w_in[j], hy_conv_w[j].astype(F32), hy_conv_b[j].astype(F32),
                                 hy_f_w1[j], hy_f_b1[j], hy_f_w2[j], hy_f_b2[j], hy_f_w3[j], hy_f_b3[j],
                                 hy_f_w4[j], hy_f_freq[j], hy_f_bias[j], not last)
            w_out = hy_w_out[j]
        w_out = w_out.astype(BF16)
        w1 = mlp_w1[i].astype(BF16)
        w2 = mlp_w2[i].astype(BF16)
        nsc_l, nsh_l = (zero, zero) if last else (mod_l(i + 1, 1), mod_l(i + 1, 0))
        xl, ul = matmul_res_ln(yl, w_out, xl, mod_l(i, 2), ln_g[i, 0], ln_b[i, 0], mod_l(i, 4), mod_l(i, 3))
        xl, ul = mlp_res_ln(ul, w1, w2, xl, mod_l(i, 5), ln_g[i, 1], ln_b[i, 1], nsc_l, nsh_l)
        if not last:
            xc, uc = matmul_res_ln(yc, w_out, xc, mod_c(i, 2), ln_g[i, 0], ln_b[i, 0], mod_c(i, 4), mod_c(i, 3))
            xc, uc = mlp_res_ln(uc, w1, w2, xc, mod_c(i, 5), ln_g[i, 1], ln_b[i, 1],
                                mod_c(i + 1, 1), mod_c(i + 1, 0))
    return xl.astype(x.dtype)
```

```python
import functools
import math

import numpy as np
import jax
import jax.numpy as jnp
from jax import lax
from jax.experimental import pallas as pl
from jax.experimental.pallas import tpu as pltpu

F32 = jnp.float32
BF16 = jnp.bfloat16
HIGHEST = lax.Precision.HIGHEST

D_MODEL = 2048
DEPTH = 4
GRID_W = 64
N_MIXERS = 3
MIXER_OF_LAYER = tuple(i % N_MIXERS for i in range(DEPTH))
ALPHA = (2.0 * DEPTH) ** 0.25
LN_EPS = 1e-5
RMS_EPS = 1e-5
N_MOD = 6
MLP_HIDDEN = 4 * D_MODEL

D_INNER = 2 * D_MODEL
SSM_HEAD_DIM = 64
SSM_HEADS = D_INNER // SSM_HEAD_DIM
SSM_GROUPS = 8
SSM_HPG = SSM_HEADS // SSM_GROUPS
SSM_STATE = 128
SSM_CONV_W = 5
SSM_CHUNK = 128
SSM_XBC = D_INNER + 2 * SSM_GROUPS * SSM_STATE
SSM_GW = D_INNER // SSM_GROUPS
SSM_SPLIT_ROWS = 6 * SSM_HPG
SSM_GPS = 8
LOG2E = math.log2(math.e)

ATTN_HEAD_DIM = 64
ATTN_Q_HEADS = D_MODEL // ATTN_HEAD_DIM
ATTN_KV_HEADS = 4
ATTN_GROUP = ATTN_Q_HEADS // ATTN_KV_HEADS
ATTN_Q_W = ATTN_Q_HEADS * ATTN_HEAD_DIM
ATTN_KV_W = ATTN_KV_HEADS * ATTN_HEAD_DIM
WINDOW = 128
ATTN_BLOCK = 128
ROPE_BASE = 10000.0
ATTN_MASKED = -1e30

HYENA_ORDER = 2
HYENA_SHORT_W = 3
HYENA_EMB = 33
HYENA_FILTER_W = 64
HYENA_DECAY_FAST = 0.3
HYENA_DECAY_SLOW = 1.5
HYENA_DECAY_TARGET = 1e-2

LANE = 128
FFT_SLAB = 128
FFT_SUB = 16
COND_ROWS = 16
VMEM_LIMIT_MB = 56


def _cparams(sem, vmem_mb=VMEM_LIMIT_MB):
    return pltpu.CompilerParams(dimension_semantics=sem,
                                vmem_limit_bytes=vmem_mb * 1024 * 1024)


def _dot(a, b, precision=None):
    return jnp.dot(a, b, preferred_element_type=F32, precision=precision)


def _dot_nt(a, b):
    return lax.dot_general(a, b, (((1,), (1,)), ((), ())), preferred_element_type=F32)


def _dot_tn(a, b):
    return lax.dot_general(a, b, (((0,), (0,)), ((), ())), preferred_element_type=F32)


def _silu(x):
    return x * jax.nn.sigmoid(x)


def _split_parts(v, parts):
    out = []
    for _ in range(parts):
        piece = v.astype(BF16).astype(F32)
        out.append(piece)
        v = v - piece
    return out


def _ada_kernel(c_ref, w_ref, b_ref, o_ref):
    s = _silu(c_ref[...])
    o_ref[0] = _dot(s, w_ref[0], HIGHEST) + b_ref[0]


def ada_mods(cond, ada_w, ada_b):
    depth, d, n6 = ada_w.shape
    tn = 512
    return pl.pallas_call(
        _ada_kernel,
        grid=(depth, n6 // tn),
        in_specs=[pl.BlockSpec((COND_ROWS, d), lambda l, j: (0, 0)),
                  pl.BlockSpec((1, d, tn), lambda l, j: (l, 0, j)),
                  pl.BlockSpec((1, 1, tn), lambda l, j: (l, 0, j))],
        out_specs=pl.BlockSpec((1, COND_ROWS, tn), lambda l, j: (l, 0, j)),
        out_shape=jax.ShapeDtypeStruct((depth, COND_ROWS, n6), F32),
        compiler_params=_cparams(("arbitrary", "arbitrary")),
        name="ada_mods",
    )(cond, ada_w, ada_b.reshape(depth, 1, n6))


def _modulate_kernel(x_ref, sc_ref, sh_ref, o_ref):
    o_ref[0] = (x_ref[0] * (1.0 + sc_ref[0]) + sh_ref[0]).astype(o_ref.dtype)


def modulate(x, sc, sh):
    b, n, d = x.shape
    tm = min(n, 1024)
    row = pl.BlockSpec((1, tm, d), lambda i, j: (i, j, 0))
    vec = pl.BlockSpec((1, 1, d), lambda i, j: (i, 0, 0))
    return pl.pallas_call(
        _modulate_kernel, grid=(b, n // tm),
        in_specs=[row, vec, vec], out_specs=row,
        out_shape=jax.ShapeDtypeStruct((b, n, d), BF16),
        compiler_params=_cparams(("arbitrary", "arbitrary")),
        name="modulate",
    )(x, sc, sh)


def _mm_kernel(a_ref, w_ref, o_ref):
    o_ref[0] = _dot(a_ref[0], w_ref[...]).astype(o_ref.dtype)


def matmul(a, w, out_dtype, tn):
    b, n, k = a.shape
    nn = w.shape[1]
    tm = min(n, 1024)
    tn = min(tn, nn)
    return pl.pallas_call(
        _mm_kernel, grid=(b, n // tm, nn // tn),
        in_specs=[pl.BlockSpec((1, tm, k), lambda i, j, l: (i, j, 0)),
                  pl.BlockSpec((k, tn), lambda i, j, l: (0, l))],
        out_specs=pl.BlockSpec((1, tm, tn), lambda i, j, l: (i, j, l)),
        out_shape=jax.ShapeDtypeStruct((b, n, nn), out_dtype),
        compiler_params=_cparams(("arbitrary", "arbitrary", "arbitrary")),
        name="matmul",
    )(a, w)


def _res_ln(x, y, gate, g, b):
    h = ALPHA * x + gate * y
    mu = jnp.mean(h, axis=-1, keepdims=True)
    hc = h - mu
    var = jnp.mean(hc * hc, axis=-1, keepdims=True)
    return hc * lax.rsqrt(var + LN_EPS) * g + b


def _mm_res_ln_kernel(a_ref, w_ref, x_ref, gate_ref, g_ref, b_ref, sc_ref, sh_ref, xo_ref, uo_ref):
    xn = _res_ln(x_ref[0], _dot(a_ref[0], w_ref[...]), gate_ref[0], g_ref[...], b_ref[...])
    xo_ref[0] = xn
    uo_ref[0] = (xn * (1.0 + sc_ref[0]) + sh_ref[0]).astype(uo_ref.dtype)


def matmul_res_ln(a, w, x, gate, ln_g, ln_b, sc_next, sh_next):
    b, n, kk = a.shape
    d = w.shape[1]
    tm = min(n, 512)
    row = pl.BlockSpec((1, tm, d), lambda i, j: (i, j, 0))
    vec = pl.BlockSpec((1, 1, d), lambda i, j: (i, 0, 0))
    par = pl.BlockSpec((1, d), lambda i, j: (0, 0))
    return pl.pallas_call(
        _mm_res_ln_kernel, grid=(b, n // tm),
        in_specs=[pl.BlockSpec((1, tm, kk), lambda i, j: (i, j, 0)),
                  pl.BlockSpec((kk, d), lambda i, j: (0, 0), pipeline_mode=pl.Buffered(1)),
                  row, vec, par, par, vec, vec],
        out_specs=[row, row],
        out_shape=[jax.ShapeDtypeStruct((b, n, d), F32), jax.ShapeDtypeStruct((b, n, d), BF16)],
        compiler_params=_cparams(("arbitrary", "arbitrary")),
        name="matmul_res_ln",
    )(a, w, x, gate, ln_g.reshape(1, d), ln_b.reshape(1, d), sc_next, sh_next)


def _mlp_kernel(u_ref, w1_ref, w2_ref, x_ref, gate_ref, g_ref, b_ref, sc_ref, sh_ref,
                xo_ref, uo_ref, acc_ref):
    k = pl.program_id(2)

    @pl.when(k == 0)
    def _():
        acc_ref[...] = jnp.zeros_like(acc_ref)

    h = jnp.maximum(_dot(u_ref[0], w1_ref[...]), 0.0)
    acc_ref[...] += _dot((h * h).astype(BF16), w2_ref[...])

    @pl.when(k == pl.num_programs(2) - 1)
    def _():
        xn = _res_ln(x_ref[0], acc_ref[...], gate_ref[0], g_ref[...], b_ref[...])
        xo_ref[0] = xn
        uo_ref[0] = (xn * (1.0 + sc_ref[0]) + sh_ref[0]).astype(uo_ref.dtype)


def mlp_res_ln(u, w1, w2, x, gate, ln_g, ln_b, sc_next, sh_next):
    b, n, d = u.shape
    hid = w1.shape[1]
    tm = min(n, 512)
    th = 1024
    row = pl.BlockSpec((1, tm, d), lambda i, j, k: (i, j, 0))
    vec = pl.BlockSpec((1, 1, d), lambda i, j, k: (i, 0, 0))
    par = pl.BlockSpec((1, d), lambda i, j, k: (0, 0))
    return pl.pallas_call(
        _mlp_kernel, grid=(b, n // tm, hid // th),
        in_specs=[row,
                  pl.BlockSpec((d, th), lambda i, j, k: (0, k)),
                  pl.BlockSpec((th, d), lambda i, j, k: (k, 0)),
                  row, vec, par, par, vec, vec],
        out_specs=[row, row],
        out_shape=[jax.ShapeDtypeStruct((b, n, d), F32), jax.ShapeDtypeStruct((b, n, d), BF16)],
        scratch_shapes=[pltpu.VMEM((tm, d), F32)],
        compiler_params=_cparams(("arbitrary", "arbitrary", "arbitrary")),
        name="mlp_res_ln",
    )(u, w1, w2, x, gate, ln_g.reshape(1, d), ln_b.reshape(1, d), sc_next, sh_next)


def _dwconv_kernel(x_ref, w_ref, b_ref, o_ref, *, width, act):
    x = x_ref[0].astype(F32)
    n = x.shape[0]
    half = width // 2
    rows = lax.broadcasted_iota(jnp.int32, x.shape, 0)
    acc = x * w_ref[half:half + 1, :] + b_ref[...]
    for k in range(width):
        off = k - half
        if off == 0:
            continue
        shifted = pltpu.roll(x, (-off) % n, axis=0)
        valid = (rows >= -off) if off < 0 else (rows < n - off)
        acc = acc + jnp.where(valid, shifted, 0.0) * w_ref[k:k + 1, :]
    if act:
        acc = _silu(acc)
    o_ref[0] = acc.astype(o_ref.dtype)


def dwconv(x, w, bias, col0, ncols, act, out_dtype=BF16):
    b, n, _ = x.shape
    width = w.shape[0]
    tc = 256 if n > 1024 else 512
    tc = min(tc, ncols)
    cb0 = col0 // tc
    return pl.pallas_call(
        functools.partial(_dwconv_kernel, width=width, act=act),
        grid=(b, ncols // tc),
        in_specs=[pl.BlockSpec((1, n, tc), lambda i, j: (i, 0, cb0 + j)),
                  pl.BlockSpec((width, tc), lambda i, j: (0, j)),
                  pl.BlockSpec((1, tc), lambda i, j: (0, j))],
        out_specs=pl.BlockSpec((1, n, tc), lambda i, j: (i, 0, j)),
        out_shape=jax.ShapeDtypeStruct((b, n, ncols), out_dtype),
        compiler_params=_cparams(("arbitrary", "arbitrary")),
        name="dwconv",
    )(x, w, bias.reshape(1, ncols))


def _dtprep_kernel(raw_ref, bias_ref, a_ref, at_ref, r0_ref, ft_ref):
    x = raw_ref[0] + bias_ref[...]
    dt = jnp.maximum(x, 0.0) + jnp.log1p(jnp.exp(-jnp.abs(x)))
    adt = dt * a_ref[...]
    q = x.shape[0]
    r = lax.broadcasted_iota(jnp.int32, (q, q), 0)
    c = lax.broadcasted_iota(jnp.int32, (q, q), 1)
    fwd = _dot((c <= r).astype(F32), adt, HIGHEST)
    bwd = _dot((c >= r).astype(F32), adt, HIGHEST)
    lane = lax.broadcasted_iota(jnp.int32, x.shape, 1)
    cum = jnp.where(lane < SSM_HEADS, fwd, bwd)
    tot = jnp.where(lane[0:1] < SSM_HEADS, cum[q - 1:q], cum[0:1])
    c3 = _split_parts((cum * LOG2E).T, 3)
    fparts = [s for v in (dt, jnp.exp(cum), jnp.exp(tot - cum) * dt) for s in _split_parts(v.T, 2)]
    ones = jnp.ones((3 * SSM_HPG, q), F32)
    for d in range(2):
        for g in range(SSM_GROUPS):
            rows = slice(d * SSM_HEADS + g * SSM_HPG, d * SSM_HEADS + (g + 1) * SSM_HPG)
            at_ref[0, 0, d, g] = jnp.concatenate([s[rows] for s in c3] + [ones], axis=0).astype(BF16)
            r0_ref[0, 0, d, g] = jnp.concatenate([ones] + [-s[rows] for s in c3], axis=0).astype(BF16)
            ft_ref[0, 0, d, g] = jnp.concatenate([s[rows] for s in fparts], axis=0).astype(BF16)


def ssd_dtprep(dt_raw, dt_bias, a_log):
    b, n, w = dt_raw.shape
    q = SSM_CHUNK
    nc = n // q
    a = -jnp.exp(a_log.astype(F32)).reshape(1, w)
    par = pl.BlockSpec((1, w), lambda i, j: (0, 0))
    oshape = (b, nc, 2, SSM_GROUPS, SSM_SPLIT_ROWS, q)
    ospec = pl.BlockSpec((1, 1) + oshape[2:], lambda i, j: (i, j, 0, 0, 0, 0))
    return pl.pallas_call(
        _dtprep_kernel, grid=(b, nc),
        in_specs=[pl.BlockSpec((1, q, w), lambda i, j: (i, j, 0)), par, par],
        out_specs=[ospec] * 3,
        out_shape=[jax.ShapeDtypeStruct(oshape, BF16)] * 3,
        compiler_params=_cparams(("arbitrary", "arbitrary")),
        name="ssd_dtprep",
    )(dt_raw, dt_bias.reshape(1, w).astype(F32), a)


def _ssd_scan_kernel(*refs, reverse, finish):
    if finish:
        (x_ref, b_ref, c_ref, at_ref, r0_ref, ft_ref, bmask_ref, eexp_ref, h0_ref,
         yf_ref, z_ref, dsk_ref, ng_ref, y_ref, hout_ref, h_scr) = refs
    else:
        (x_ref, b_ref, c_ref, at_ref, r0_ref, ft_ref, bmask_ref, eexp_ref, h0_ref,
         y_ref, hout_ref, h_scr) = refs
    step = pl.program_id(2)

    @pl.when(step == 0)
    def _():
        h_scr[...] = h0_ref[0]

    q = SSM_CHUNK
    p = SSM_HEAD_DIM
    gw = SSM_GW
    ns = SSM_STATE
    ri = lax.broadcasted_iota(jnp.int32, (q, q), 0)
    ci = lax.broadcasted_iota(jnp.int32, (q, q), 1)
    keep = (ci >= ri) if reverse else (ci <= ri)
    left = lax.broadcasted_iota(jnp.int32, (q, 2 * p), 1) < p

    for gi in range(SSM_GPS):
        gcols = slice(gi * gw, (gi + 1) * gw)
        x = x_ref[0, :, gcols].astype(F32)
        bm = b_ref[0, :, gi * ns:(gi + 1) * ns]
        cm = c_ref[0, :, gi * ns:(gi + 1) * ns]
        cb = jnp.where(keep, _dot_nt(cm, bm), 0.0)
        rhs = jnp.concatenate([r0_ref[0, 0, 0, gi]] * SSM_HPG, axis=1) * bmask_ref[...]
        diff = _dot_tn(at_ref[0, 0, 0, gi], rhs)
        ft = ft_ref[0, 0, 0, gi]
        fac = _dot_tn(ft, eexp_ref[...])
        ec = fac[:, :gw]
        xw = (x * fac[:, gw:]).astype(BF16)
        etot = ec[0:1] if reverse else ec[q - 1:q]
        ht = h_scr[gi]
        yoff = _dot(cm, ht.astype(BF16)) * ec
        dtj = ft[0:SSM_HPG].astype(F32) + ft[SSM_HPG:2 * SSM_HPG].astype(F32)

        ys = []
        for pr in range(SSM_HPG // 2):
            cols = slice(pr * 2 * p, (pr + 1) * 2 * p)
            xp = x[:, cols]
            yp = yoff[:, cols]
            for hd, xm in ((2 * pr, jnp.where(left, xp, 0.0)), (2 * pr + 1, jnp.where(left, 0.0, xp))):
                dec = jnp.exp2(jnp.minimum(diff[:, hd * q:(hd + 1) * q], 0.0))
                yp = yp + _dot((cb * dtj[hd:hd + 1] * dec).astype(BF16), xm.astype(BF16))
            ys.append(yp)
        y = jnp.concatenate(ys, axis=1)
        h_scr[gi] = ht * etot + _dot_tn(bm, xw)

        if finish:
            yt = y + yf_ref[0, :, gcols] + dsk_ref[:, gcols] * x
            yt = yt * _silu(z_ref[0, :, gcols].astype(F32))
            ms = jnp.mean(yt * yt, axis=-1, keepdims=True)
            y_ref[0, :, gcols] = (yt * lax.rsqrt(ms + RMS_EPS) * ng_ref[:, gcols]).astype(y_ref.dtype)
        else:
            y_ref[0, :, gcols] = y

    @pl.when(step == pl.num_programs(2) - 1)
    def _():
        hout_ref[0] = h_scr[...]


def _ssd_tables():
    q, hg, p = SSM_CHUNK, SSM_HPG, SSM_HEAD_DIM
    r = np.arange(SSM_SPLIT_ROWS)[:, None]
    bmask = (r % hg == np.arange(hg * q)[None, :] // q)
    c = np.arange(2 * SSM_GW)[None, :]
    eexp = (r // (2 * hg) == 1 + c // SSM_GW) & (r % hg == (c % SSM_GW) // p)
    return jnp.asarray(bmask, BF16), jnp.asarray(eexp, BF16)


def ssd_scan(xbc, at, r0, ft, h0, *, reverse, fin=None):
    b, n, _ = xbc.shape
    q, g, gw, ns = SSM_CHUNK, SSM_GROUPS, SSM_GW, SSM_STATE
    sr = SSM_SPLIT_ROWS
    nc = n // q
    bmask, eexp = _ssd_tables()

    def ch(k):
        return nc - 1 - k if reverse else k

    gps = SSM_GPS
    bcol = D_INNER // (gps * ns)
    direction = 1 if reverse else 0
    opspec = pl.BlockSpec((1, 1, 1, gps, sr, q), lambda i, j, k: (i, ch(k), direction, j, 0, 0))
    in_specs = [
        pl.BlockSpec((1, q, gps * gw), lambda i, j, k: (i, ch(k), j)),
        pl.BlockSpec((1, q, gps * ns), lambda i, j, k: (i, ch(k), bcol + j)),
        pl.BlockSpec((1, q, gps * ns), lambda i, j, k: (i, ch(k), bcol + g // gps + j)),
        opspec, opspec, opspec,
        pl.BlockSpec(bmask.shape, lambda i, j, k: (0, 0)),
        pl.BlockSpec(eexp.shape, lambda i, j, k: (0, 0)),
        pl.BlockSpec((1, gps, ns, gw), lambda i, j, k: (i, j, 0, 0)),
    ]
    args = [xbc, xbc, xbc, at, r0, ft, bmask, eexp, h0]
    yspec = pl.BlockSpec((1, q, gps * gw), lambda i, j, k: (i, ch(k), j))
    if fin is not None:
        y_other, z, dsk, ng = fin
        in_specs += [yspec, yspec,
                     pl.BlockSpec((1, gps * gw), lambda i, j, k: (0, j)),
                     pl.BlockSpec((1, gps * gw), lambda i, j, k: (0, j))]
        args += [y_other, z, dsk, ng]
    return pl.pallas_call(
        functools.partial(_ssd_scan_kernel, reverse=reverse, finish=fin is not None),
        grid=(b, g // gps, nc),
        in_specs=in_specs,
        out_specs=[yspec, pl.BlockSpec((1, gps, ns, gw), lambda i, j, k: (i, j, 0, 0))],
        out_shape=[jax.ShapeDtypeStruct((b, n, D_INNER), BF16 if fin is not None else F32),
                   jax.ShapeDtypeStruct((b, g, ns, gw), F32)],
        scratch_shapes=[pltpu.VMEM((gps, ns, gw), F32)],
        compiler_params=_cparams(("arbitrary", "arbitrary", "arbitrary")),
        name="ssd_scan_bwd" if reverse else "ssd_scan_fwd",
    )(*args)


def ssd_mixer(u_lat, u_ctx, w_in, conv_w, conv_b, dt_bias, a_log, d_skip, norm_g, ctx_out):
    g, hg = SSM_GROUPS, SSM_HPG
    w_z = w_in[:, :D_INNER].astype(BF16)
    w_xbc = w_in[:, D_INNER:D_INNER + SSM_XBC].astype(BF16)
    w_dt = w_in[:, D_INNER + SSM_XBC:].astype(BF16)
    dsk = jnp.repeat(d_skip.astype(F32), SSM_HEAD_DIM).reshape(1, D_INNER)
    ng = norm_g.astype(F32).reshape(1, D_INNER)

    def project(u):
        b, n, _ = u.shape
        z = matmul(u, w_z, BF16, 1024)
        xbc = dwconv(matmul(u, w_xbc, BF16, 1024), conv_w, conv_b, 0, SSM_XBC, act=True)
        at, r0, ft = ssd_dtprep(matmul(u, w_dt, F32, 2 * SSM_HEADS), dt_bias, a_log)
        return z, xbc, at, r0, ft

    def bidir(proj, h_f, h_b, want_y):
        z, xbc, at, r0, ft = proj
        y_f, s_f = ssd_scan(xbc, at, r0, ft, h_f, reverse=False)
        fin = (y_f, z, dsk, ng) if want_y else None
        y, s_b = ssd_scan(xbc, at, r0, ft, h_b, reverse=True, fin=fin)
        return y, s_f, s_b

    h0 = jnp.zeros((u_lat.shape[0], g, SSM_STATE, SSM_GW), F32)
    yc, hc_f, hc_b = bidir(project(u_ctx), h0, h0, ctx_out)
    yl, _, _ = bidir(project(u_lat), hc_f, hc_b, True)
    return yl, (yc if ctx_out else None)


def _rope_kernel(x_ref, cos_ref, sin_ref, o_ref):
    cos = cos_ref[...]
    sin = sin_ref[...]
    half = ATTN_HEAD_DIM // 2
    first = (lax.broadcasted_iota(jnp.int32, cos.shape, 1) % ATTN_HEAD_DIM) < half
    for t in range(x_ref.shape[2] // LANE):
        cols = slice(t * LANE, (t + 1) * LANE)
        x = x_ref[0, :, cols].astype(F32)
        partner = jnp.where(first, pltpu.roll(x, LANE - half, axis=1), pltpu.roll(x, half, axis=1))
        o_ref[0, :, cols] = (x * cos + partner * sin).astype(o_ref.dtype)


def rope(qkv, cos_t, sin_t):
    b, n, _ = qkv.shape
    w = ATTN_Q_W + ATTN_KV_W
    tm = 512
    return pl.pallas_call(
        _rope_kernel, grid=(b, n // tm),
        in_specs=[pl.BlockSpec((1, tm, w), lambda i, j: (i, j, 0)),
                  pl.BlockSpec((tm, LANE), lambda i, j: (j, 0)),
                  pl.BlockSpec((tm, LANE), lambda i, j: (j, 0))],
        out_specs=pl.BlockSpec((1, tm, w), lambda i, j: (i, j, 0)),
        out_shape=jax.ShapeDtypeStruct((b, n, w), BF16),
        compiler_params=_cparams(("arbitrary", "arbitrary")),
        name="rope",
    )(qkv, cos_t, sin_t)


def _attn_kernel(*refs, local, seq_len):
    if local:
        (q_ref, k0_ref, k1_ref, k2_ref, v0_ref, v1_ref, v2_ref, kc_ref, vc_ref, sink_ref, band_ref,
         o_ref, qx_ref, kx_ref, vx_ref) = refs
        kv_blocks = ((k0_ref, v0_ref), (k1_ref, v1_ref), (k2_ref, v2_ref))
    else:
        q_ref, kc_ref, vc_ref, sink_ref, o_ref, qx_ref, kx_ref, vx_ref = refs
        kv_blocks = ()
    blk, dh, grp = ATTN_BLOCK, ATTN_HEAD_DIM, ATTN_GROUP
    rows = grp * blk
    nloc = len(kv_blocks) * blk
    pad = LANE - dh

    @pl.when((pl.program_id(0) == 0) & (pl.program_id(1) == 0))
    def _():
        r = lax.broadcasted_iota(jnp.int32, (rows, LANE), 0) % blk
        c = lax.broadcasted_iota(jnp.int32, (rows, LANE), 1)
        cpad = lax.broadcasted_iota(jnp.int32, (rows, pad), 1)
        for kv in range(ATTN_KV_HEADS):
            qx_ref[kv, :, 0:LANE] = jnp.where(r == c, 1.0, 0.0).astype(BF16)
            qx_ref[kv, :, LANE + dh:] = jnp.where(cpad == 0, 1.0, 0.0).astype(BF16)
            kx_ref[kv] = jnp.zeros(kx_ref.shape[1:], BF16)
            if local:
                kx_ref[kv, 0:nloc, 0:LANE] = band_ref[...]
            vx_ref[kv, :, dh:] = jnp.ones((vx_ref.shape[1], pad), BF16)

    if local:
        sj = lax.broadcasted_iota(jnp.int32, (nloc, pad), 0)
        lane = lax.broadcasted_iota(jnp.int32, (nloc, pad), 1)
        kpos = (pl.program_id(1) - 1) * blk + sj
        outside = (kpos < 0) | (kpos >= seq_len)
        edge = jnp.where((lane == 0) & outside, ATTN_MASKED, 0.0).astype(BF16)

    for kv in range(ATTN_KV_HEADS):
        hc = slice(kv * dh, (kv + 1) * dh)
        if local:
            kx_ref[kv, 0:nloc, LANE + dh:] = edge
        for g in range(grp):
            c0 = (kv * grp + g) * dh
            qx_ref[kv, g * blk:(g + 1) * blk, LANE:LANE + dh] = q_ref[0, :, c0:c0 + dh]
        for t, (k_ref, v_ref) in enumerate(kv_blocks):
            kx_ref[kv, t * blk:(t + 1) * blk, LANE:LANE + dh] = k_ref[0, :, hc]
            vx_ref[kv, t * blk:(t + 1) * blk, 0:dh] = v_ref[0, :, hc]
        kx_ref[kv, nloc:, LANE:LANE + dh] = kc_ref[0, :, hc]
        vx_ref[kv, nloc:, 0:dh] = vc_ref[0, :, hc]
    heads = range(ATTN_KV_HEADS)
    s = [_dot_nt(qx_ref[kv], kx_ref[kv]) for kv in heads]
    m = [jnp.maximum(jnp.max(s[kv], axis=-1, keepdims=True), sink_ref[kv]) for kv in heads]
    p = [jnp.exp2(s[kv] - m[kv]).astype(BF16) for kv in heads]
    oe = [_dot(p[kv], vx_ref[kv]) for kv in heads]
    for kv in heads:
        den = pltpu.roll(oe[kv], dh, axis=1) + jnp.exp2(sink_ref[kv] - m[kv])
        o = (oe[kv] / den).astype(o_ref.dtype)
        for g in range(grp):
            c0 = (kv * grp + g) * dh
            o_ref[0, :, c0:c0 + dh] = o[g * blk:(g + 1) * blk, 0:dh]


def attention(qk, qkv, qkv_ctx, sink_col, *, local):
    b, n, _ = qk.shape
    nctx = qkv_ctx.shape[1]
    blk = ATTN_BLOCK
    nblk = n // blk
    kcb = ATTN_Q_W // ATTN_KV_W
    vcb = kcb + 1
    qspec = pl.BlockSpec((1, blk, ATTN_Q_W), lambda i, j: (i, j, 0))
    ctx_k = pl.BlockSpec((1, nctx, ATTN_KV_W), lambda i, j: (i, 0, kcb))
    ctx_v = pl.BlockSpec((1, nctx, ATTN_KV_W), lambda i, j: (i, 0, vcb))
    sspec = pl.BlockSpec(sink_col.shape, lambda i, j: (0, 0, 0))
    if local:
        def win(cb, off):
            return pl.BlockSpec((1, blk, ATTN_KV_W),
                                lambda i, j: (i, jnp.clip(j + off, 0, nblk - 1), cb))
        sidx = np.arange(3 * blk)[:, None]
        ridx = np.arange(blk)[None, :]
        band = jnp.asarray(np.where(np.abs(sidx - blk - ridx) <= WINDOW, 0.0, ATTN_MASKED), BF16)
        in_specs = [qspec, win(kcb, -1), win(kcb, 0), win(kcb, 1),
                    win(vcb, -1), win(vcb, 0), win(vcb, 1), ctx_k, ctx_v, sspec,
                    pl.BlockSpec(band.shape, lambda i, j: (0, 0))]
        args = [qk, qk, qk, qk, qkv, qkv, qkv, qkv_ctx, qkv_ctx, sink_col, band]
        nkeys = 3 * blk + nctx
    else:
        in_specs = [qspec, ctx_k, ctx_v, sspec]
        args = [qk, qkv_ctx, qkv_ctx, sink_col]
        nkeys = nctx
    return pl.pallas_call(
        functools.partial(_attn_kernel, local=local, seq_len=n),
        grid=(b, nblk),
        in_specs=in_specs,
        out_specs=qspec,
        out_shape=jax.ShapeDtypeStruct((b, n, ATTN_Q_W), BF16),
        scratch_shapes=[pltpu.VMEM((ATTN_KV_HEADS, ATTN_GROUP * blk, 2 * LANE), BF16),
                        pltpu.VMEM((ATTN_KV_HEADS, nkeys, 2 * LANE), BF16),
                        pltpu.VMEM((ATTN_KV_HEADS, nkeys, LANE), BF16)],
        compiler_params=_cparams(("arbitrary", "arbitrary")),
        name="attention_local" if local else "attention_ctx",
    )(*args)


def _rope_tables(n):
    rows = n // GRID_W
    row_id = np.repeat(np.arange(rows, dtype=np.float32), GRID_W)
    col_id = np.tile(np.arange(GRID_W, dtype=np.float32), rows)
    pairs = ATTN_HEAD_DIM // 4
    inv = (np.float32(ROPE_BASE) ** (-np.arange(pairs, dtype=np.float32) / np.float32(pairs))).astype(np.float32)
    ang = np.concatenate([row_id[:, None] * inv, col_id[:, None] * inv], axis=-1).astype(np.float32)
    cos, sin = np.cos(ang.astype(np.float64)), np.sin(ang.astype(np.float64))
    reps = LANE // ATTN_HEAD_DIM
    cos_t = np.tile(np.concatenate([cos, cos], axis=-1), (1, reps))
    sin_t = np.tile(np.concatenate([-sin, sin], axis=-1), (1, reps))
    return jnp.asarray(cos_t, F32), jnp.asarray(sin_t, F32)


def attn_mixer(u_lat, u_ctx, w_qkv, sink, ctx_out):
    dh = ATTN_HEAD_DIM
    perm = np.concatenate([np.arange(0, dh, 2), np.arange(1, dh, 2)])
    nqk = ATTN_Q_HEADS + ATTN_KV_HEADS
    cols = (np.arange(nqk)[:, None] * dh + perm[None, :]).reshape(-1)
    cols = np.concatenate([cols, np.arange(nqk * dh, nqk * dh + ATTN_KV_W)])
    colscale = np.where(np.arange(cols.shape[0]) < ATTN_Q_W, dh ** -0.5 * LOG2E, 1.0).astype(np.float32)
    w = (w_qkv[:, cols] * colscale).astype(BF16)
    sink_col = jnp.repeat(sink.astype(F32).reshape(ATTN_KV_HEADS, ATTN_GROUP) * LOG2E, ATTN_BLOCK, axis=1)
    sink_col = sink_col.reshape(ATTN_KV_HEADS, ATTN_GROUP * ATTN_BLOCK, 1)

    qkv_l = matmul(u_lat, w, BF16, 512)
    qkv_c = matmul(u_ctx, w, BF16, 512)
    cos_t, sin_t = _rope_tables(u_lat.shape[1])
    qk_l = rope(qkv_l, cos_t, sin_t)
    o_l = attention(qk_l, qkv_l, qkv_c, sink_col, local=True)
    o_c = attention(qkv_c, qkv_c, qkv_c, sink_col, local=False) if ctx_out else None
    return o_l, o_c


def _filter_kernel(z_ref, t_ref, w1_ref, b1_ref, w2_ref, b2_ref, w3_ref, b3_ref, fr_ref,
                   w4f_ref, w4b_ref, dl_ref, o_ref, h_scr, *, seq_len):
    @pl.when((pl.program_id(1) == 0) & (pl.program_id(2) == 0))
    def _():
        fr = fr_ref[...]
        h = jnp.sin(fr * (_dot(z_ref[...], w1_ref[...], HIGHEST) + b1_ref[...]))
        h = jnp.sin(fr * (_dot(h, w2_ref[...], HIGHEST) + b2_ref[...]))
        h_scr[...] = jnp.sin(fr * (_dot(h, w3_ref[...], HIGHEST) + b3_ref[...]))

    h = h_scr[...]
    tr = h.shape[0]
    r = pl.program_id(0) * tr + lax.broadcasted_iota(jnp.int32, (tr, 1), 0)
    hf = _dot(h, w4f_ref[...], HIGHEST)
    hb = _dot(h, w4b_ref[...], HIGHEST)
    val = jnp.where(r < seq_len, hf, hb) * jnp.exp(-t_ref[...] * dl_ref[...])
    o_ref[0] = jnp.where(r == seq_len, 0.0, val)


def hyena_filters(n, w1, b1, w2, b2, w3, b3, w4, freq):
    d = D_MODEL
    fw = HYENA_FILTER_W
    lag = np.concatenate([np.arange(n), [0], np.arange(n - 1, 0, -1)])
    t = np.linspace(0.0, 1.0, n, dtype=np.float32)[:, None]
    bands = (HYENA_EMB - 1) // 2
    wv = (np.float32(2.0 * math.pi) * np.arange(n, dtype=np.float32)[:, None] / np.float32(n)).astype(np.float32)
    f = np.linspace(1e-4, bands - 1, bands, dtype=np.float32)[None, :]
    fwv = (f * wv).astype(np.float32).astype(np.float64)
    z = np.concatenate([t, np.cos(fwv), -np.sin(fwv)], axis=-1).astype(np.float32)
    zpad = np.zeros((2 * n, fw), np.float32)
    zpad[:, :HYENA_EMB] = z[lag]
    tfull = t[lag]
    max_decay = math.log(HYENA_DECAY_TARGET) / HYENA_DECAY_FAST
    min_decay = math.log(HYENA_DECAY_TARGET) / HYENA_DECAY_SLOW
    deltas = np.abs(np.linspace(min_decay, max_decay, d, dtype=np.float32))[None, :]
    w1p = jnp.zeros((fw, fw), F32).at[:HYENA_EMB].set(w1.astype(F32))
    tr = min(2 * n, 512)
    td = 512
    nd = d // td
    par = lambda shape: pl.BlockSpec(shape, lambda i, o, j: (0, 0))
    return pl.pallas_call(
        functools.partial(_filter_kernel, seq_len=n),
        grid=(2 * n // tr, HYENA_ORDER, nd),
        in_specs=[pl.BlockSpec((tr, fw), lambda i, o, j: (i, 0)),
                  pl.BlockSpec((tr, 1), lambda i, o, j: (i, 0)),
                  par((fw, fw)), par((1, fw)), par((fw, fw)), par((1, fw)),
                  par((fw, fw)), par((1, fw)), par((1, fw)),
                  pl.BlockSpec((fw, td), lambda i, o, j: (0, (2 * o) * nd + j)),
                  pl.BlockSpec((fw, td), lambda i, o, j: (0, (2 * o + 1) * nd + j)),
                  pl.BlockSpec((1, td), lambda i, o, j: (0, j))],
        out_specs=pl.BlockSpec((1, tr, td), lambda i, o, j: (o, i, j)),
        out_shape=jax.ShapeDtypeStruct((HYENA_ORDER, 2 * n, d), F32),
        scratch_shapes=[pltpu.VMEM((tr, fw), F32)],
        compiler_params=_cparams(("arbitrary", "arbitrary", "arbitrary")),
        name="hyena_filters",
    )(jnp.asarray(zpad), jnp.asarray(tfull), w1p, b1.astype(F32).reshape(1, fw),
      w2.astype(F32), b2.astype(F32).reshape(1, fw), w3.astype(F32), b3.astype(F32).reshape(1, fw),
      freq.astype(F32).reshape(1, fw), w4.astype(F32), w4.astype(F32), jnp.asarray(deltas))


def _bmm_left_kernel(*refs, gated):
    if gated:
        f_ref, x_ref, g_ref, v_ref, bias_ref, o_ref = refs
    else:
        f_ref, x_ref, o_ref = refs
    y = _dot(f_ref[...], x_ref[0].astype(BF16))
    if gated:
        v = v_ref[0].astype(F32)
        y = g_ref[0].astype(F32) * (y + bias_ref[...] * v)
    o_ref[0] = y.astype(o_ref.dtype)


def bmm_left(f, x, out_dtype, gate=None):
    p, k, c = x.shape
    mo = f.shape[0]
    tc = min(c, 2048)
    xspec = pl.BlockSpec((1, k, tc), lambda i, j: (i, 0, j))
    ospec = pl.BlockSpec((1, mo, tc), lambda i, j: (i, 0, j))
    in_specs = [pl.BlockSpec((mo, k), lambda i, j: (0, 0)), xspec]
    args = [f, x]
    if gate is not None:
        in_specs += [ospec, ospec, pl.BlockSpec((1, tc), lambda i, j: (0, j))]
        args += list(gate)
    return pl.pallas_call(
        functools.partial(_bmm_left_kernel, gated=gate is not None),
        grid=(p, c // tc),
        in_specs=in_specs, out_specs=ospec,
        out_shape=jax.ShapeDtypeStruct((p, mo, c), out_dtype),
        compiler_params=_cparams(("arbitrary", "arbitrary")),
        name="bmm_left_gated" if gate is not None else "bmm_left",
    )(*args)


def _stride_dft_kernel(*refs, gated):
    if gated:
        f_ref, x_ref, g_ref, v_ref, bias_ref, o_ref = refs
    else:
        f_ref, x_ref, o_ref = refs
    td = x_ref.shape[-1]
    x = x_ref[0].reshape(-1, td).astype(BF16)
    y = _dot(f_ref[...], x).reshape(o_ref.shape[1:])
    if gated:
        y = g_ref[0].astype(F32) * (y + bias_ref[...] * v_ref[0].astype(F32))
    o_ref[0] = y.astype(o_ref.dtype)


def stride_dft(fk, x, out_dtype, gate=None):
    p, k, r, d = x.shape
    sub = FFT_SUB
    mo = fk.shape[0] // sub
    td = min(d, 1024)
    xspec = pl.BlockSpec((1, k, sub, td), lambda i, j, l: (i, 0, j, l))
    ospec = pl.BlockSpec((1, mo, sub, td), lambda i, j, l: (i, 0, j, l))
    in_specs = [pl.BlockSpec(fk.shape, lambda i, j, l: (0, 0)), xspec]
    args = [fk, x]
    if gate is not None:
        in_specs += [ospec, ospec, pl.BlockSpec((1, td), lambda i, j, l: (0, l))]
        args += list(gate)
    return pl.pallas_call(
        functools.partial(_stride_dft_kernel, gated=gate is not None),
        grid=(p, r // sub, d // td),
        in_specs=in_specs, out_specs=ospec,
        out_shape=jax.ShapeDtypeStruct((p, mo, r, d), out_dtype),
        compiler_params=_cparams(("arbitrary", "arbitrary", "arbitrary")),
        name="stride_dft_gated" if gate is not None else "stride_dft",
    )(*args)


def _slab_fwd_kernel(m1t_ref, t_ref, o_ref):
    r = FFT_SLAB
    t = t_ref[0, :, 0].reshape(2 * r, t_ref.shape[-1]).astype(BF16)
    xt = _dot(t.T, m1t_ref[0])
    o_ref[0, 0] = xt[:, :r]
    o_ref[1, 0] = xt[:, r:]


def slab_spectrum(m1t, t):
    _, a, r, d = t.shape
    td = min(d, 2048)
    return pl.pallas_call(
        _slab_fwd_kernel, grid=(a, d // td),
        in_specs=[pl.BlockSpec((1, 2 * r, 2 * r), lambda i, j: (i, 0, 0)),
                  pl.BlockSpec((1, 2, 1, r, td), lambda i, j: (0, 0, i, 0, j))],
        out_specs=pl.BlockSpec((2, 1, td, r), lambda i, j: (0, i, j, 0)),
        out_shape=jax.ShapeDtypeStruct((2, a, d, r), F32),
        compiler_params=_cparams(("arbitrary", "arbitrary")),
        name="slab_spectrum",
    )(m1t, t.reshape(1, 2, a, r, d))


def _slab_conv_kernel(m1t_ref, m2t_ref, h_ref, t_ref, o_ref):
    r = FFT_SLAB
    td = t_ref.shape[-1]
    t = t_ref[0, :, 0].reshape(2 * r, td)
    xt = _dot(t.T, m1t_ref[0])
    xr, xi = xt[:, :r], xt[:, r:]
    hr, hi = h_ref[0, 0], h_ref[1, 0]
    yt = jnp.concatenate([xr * hr - xi * hi, xr * hi + xi * hr], axis=1).astype(BF16)
    ut = _dot(yt, m2t_ref[0])
    o_ref[0, :, 0] = ut.T.reshape(2, r, td).astype(o_ref.dtype)


def slab_conv(m1t, m2t, h, t):
    p, _, a, r, d = t.shape
    td = min(d, 2048)
    mspec = pl.BlockSpec((1, 2 * r, 2 * r), lambda i, j, k: (i, 0, 0))
    tspec = pl.BlockSpec((1, 2, 1, r, td), lambda i, j, k: (k, 0, i, 0, j))
    return pl.pallas_call(
        _slab_conv_kernel, grid=(a, d // td, p),
        in_specs=[mspec, mspec, pl.BlockSpec((2, 1, td, r), lambda i, j, k: (0, i, j, 0)), tspec],
        out_specs=tspec,
        out_shape=jax.ShapeDtypeStruct(t.shape, BF16),
        compiler_params=_cparams(("arbitrary", "arbitrary", "arbitrary")),
        name="slab_conv",
    )(m1t, m2t, h, t)


def _direct_conv_kernel(f1_ref, f2_ref, h_ref, x_ref, g_ref, bias_ref, o_ref):
    xin = x_ref[0]
    half = f1_ref.shape[0] // 2
    x = _dot(f1_ref[...], xin)
    xr, xi = x[:half], x[half:]
    hr, hi = h_ref[0], h_ref[1]
    y = jnp.concatenate([xr * hr - xi * hi, xr * hi + xi * hr], axis=0).astype(BF16)
    u = _dot(f2_ref[...], y)
    o_ref[0] = (g_ref[0].astype(F32) * (u + bias_ref[...] * xin.astype(F32))).astype(o_ref.dtype)


def direct_conv(f1, f2, h, x, g, bias_row):
    p, n2, d = x.shape
    td = 512
    xspec = pl.BlockSpec((1, n2, td), lambda i, j: (i, 0, j))
    return pl.pallas_call(
        _direct_conv_kernel, grid=(p, d // td),
        in_specs=[pl.BlockSpec(f1.shape, lambda i, j: (0, 0)),
                  pl.BlockSpec(f2.shape, lambda i, j: (0, 0)),
                  pl.BlockSpec((2, n2, td), lambda i, j: (0, 0, j)),
                  xspec, xspec, pl.BlockSpec((1, td), lambda i, j: (0, j))],
        out_specs=xspec,
        out_shape=jax.ShapeDtypeStruct((p, n2, d), BF16),
        compiler_params=_cparams(("arbitrary", "arbitrary")),
        name="direct_conv",
    )(f1, f2, h, x, g, bias_row)


def _cplx_real_form(c):
    return np.block([[c.real, -c.imag], [c.imag, c.real]])


def _dft_tables(n):
    nn = 2 * n
    r = FFT_SLAB
    if nn <= 4 * r:
        k = np.arange(nn)[:, None]
        m = np.arange(nn)[None, :]
        fc = np.exp(-2j * np.pi * ((k * m) % nn) / nn)
        f1 = _cplx_real_form(fc[:, :n])
        f2 = _cplx_real_form(np.conj(fc.T)[:n, :] / nn)
        fh = np.concatenate([fc.real, fc.imag], axis=0)
        return dict(f1=jnp.asarray(f1, BF16), f2=jnp.asarray(f2, BF16), fh=jnp.asarray(fh, BF16))
    a = nn // r
    ka = np.arange(a)[:, None]
    aa = np.arange(a)[None, :]
    fa = np.exp(-2j * np.pi * ((ka * aa) % a) / a)
    fa_fwd = _cplx_real_form(fa[:, :a // 2])
    fa_inv = _cplx_real_form(np.conj(fa.T)[:a // 2, :] / nn)
    fa_h = np.concatenate([fa.real, fa.imag], axis=0)
    kb = np.arange(r)[None, :, None]
    bb = np.arange(r)[None, None, :]
    kk = np.arange(a)[:, None, None]
    m1c = np.exp(-2j * np.pi * ((bb * (kk + a * kb)) % nn) / nn)
    m1 = np.stack([_cplx_real_form(m1c[i]).T for i in range(a)])
    m2 = np.stack([_cplx_real_form(np.conj(m1c[i].T)).T for i in range(a)])
    eye = np.eye(FFT_SUB)
    kron = lambda f: jnp.asarray(np.kron(f, eye), BF16)
    return dict(fa_fwd=kron(fa_fwd), fa_inv=kron(fa_inv), fa_h=kron(fa_h),
                m1=jnp.asarray(m1, BF16), m2=jnp.asarray(m2, BF16))


def hyena_long_convs(x1, x2, v, hfull, f_bias):
    b, n, d = v.shape
    nn = 2 * n
    r = FFT_SLAB
    tabs = _dft_tables(n)
    p = b // 2
    bias = f_bias.astype(F32)
    if 'f1' in tabs:
        h = bmm_left(tabs['fh'], hfull, F32).reshape(HYENA_ORDER, 2, nn, d)
        pair = lambda t: t.reshape(p, nn, d)
        z = direct_conv(tabs['f1'], tabs['f2'], h[0], pair(v), pair(x1), bias[0].reshape(1, d))
        y = direct_conv(tabs['f1'], tabs['f2'], h[1], z, pair(x2), bias[1].reshape(1, d))
        return y.reshape(b, n, d)
    a = nn // r
    th = stride_dft(tabs['fa_h'], hfull.reshape(HYENA_ORDER, a, r, d), BF16)
    h = [slab_spectrum(tabs['m1'], th[o].reshape(2, a, r, d)) for o in range(HYENA_ORDER)]
    pair = lambda t: t.reshape(p, a, r, d)

    def conv(sig, gate, o):
        t = stride_dft(tabs['fa_fwd'], pair(sig), BF16)
        u = slab_conv(tabs['m1'], tabs['m2'], h[o], t.reshape(p, 2, a, r, d))
        return stride_dft(tabs['fa_inv'], u.reshape(p, 2 * a, r, d), BF16,
                          gate=(pair(gate), pair(sig), bias[o].reshape(1, d)))

    z = conv(v, x1, 0)
    y = conv(z.reshape(b, n, d), x2, 1)
    return y.reshape(b, n, d)


def hyena_mixer(u_lat, u_ctx, w_in, conv_w, conv_b, f_w1, f_b1, f_w2, f_b2, f_w3, f_b3, f_w4,
                f_freq, f_bias, ctx_out):
    d = D_MODEL
    w = w_in.astype(BF16)

    def run(u):
        n = u.shape[1]
        hfull = hyena_filters(n, f_w1, f_b1, f_w2, f_b2, f_w3, f_b3, f_w4, f_freq)
        xin = matmul(u, w, BF16, 1024)
        x1, x2, v = [dwconv(xin, conv_w[:, i * d:(i + 1) * d], conv_b[i * d:(i + 1) * d], i * d, d, act=False)
                     for i in range(3)]
        return hyena_long_convs(x1, x2, v, hfull, f_bias)

    return run(u_lat), (run(u_ctx) if ctx_out else None)


def kernel(x, c, ctx, c_ctx, ada_w, ada_b, ln_g, ln_b, mlp_w1, mlp_w2, ssd_w_in, ssd_conv_w, ssd_conv_b, ssd_dt_bias, ssd_a_log, ssd_d, ssd_norm_g, ssd_w_out, attn_w_qkv, attn_sink, attn_w_o, hy_w_in, hy_conv_w, hy_conv_b, hy_f_w1, hy_f_b1, hy_f_w2, hy_f_b2, hy_f_w3, hy_f_b3, hy_f_w4, hy_f_freq, hy_f_bias, hy_w_out):
    bsz, _, d = x.shape
    depth = ada_w.shape[0]
    assert bsz + 1 <= COND_ROWS and bsz % 2 == 0
    cond = jnp.zeros((COND_ROWS, d), F32).at[:bsz].set(c.astype(F32)).at[bsz].set(c_ctx.astype(F32))
    mods = ada_mods(cond, ada_w.astype(F32), ada_b.astype(F32))

    def mod_l(i, k):
        return mods[i, :bsz, k * d:(k + 1) * d][:, None, :]

    def mod_c(i, k):
        return jnp.broadcast_to(mods[i, bsz, k * d:(k + 1) * d][None, None, :], (bsz, 1, d))

    zero = jnp.zeros((bsz, 1, d), F32)
    xl, xc = x.astype(F32), ctx.astype(F32)
    ul = modulate(xl, mod_l(0, 1), mod_l(0, 0))
    uc = modulate(xc, mod_c(0, 1), mod_c(0, 0))
    for i in range(depth):
        last = i == depth - 1
        kind = MIXER_OF_LAYER[i]
        j = MIXER_OF_LAYER[:i].count(kind)
        if kind == 0:
            yl, yc = ssd_mixer(ul, uc, ssd_w_in[j], ssd_conv_w[j].astype(F32), ssd_conv_b[j].astype(F32),
                               ssd_dt_bias[j], ssd_a_log[j], ssd_d[j], ssd_norm_g[j], not last)
            w_out = ssd_w_out[j]
        elif kind == 1:
            yl, yc = attn_mixer(ul, uc, attn_w_qkv[j], attn_sink[j], not last)
            w_out = attn_w_o[j]
        else:
            yl, yc = hyena_mixer(ul, uc, hy_w_in[j], hy_conv_w[j].astype(F32), hy_conv_b[j].astype(F32),
                                 hy_f_w1[j], hy_f_b1[j], hy_f_w2[j], hy_f_b2[j], hy_f_w3[j], hy_f_b3[j],
                                 hy_f_w4[j], hy_f_freq[j], hy_f_bias[j], not last)
            w_out = hy_w_out[j]
        w_out = w_out.astype(BF16)
        w1 = mlp_w1[i].astype(BF16)
        w2 = mlp_w2[i].astype(BF16)
        nsc_l, nsh_l = (zero, zero) if last else (mod_l(i + 1, 1), mod_l(i + 1, 0))
        xl, ul = matmul_res_ln(yl, w_out, xl, mod_l(i, 2), ln_g[i, 0], ln_b[i, 0], mod_l(i, 4), mod_l(i, 3))
        xl, ul = mlp_res_ln(ul, w1, w2, xl, mod_l(i, 5), ln_g[i, 1], ln_b[i, 1], nsc_l, nsh_l)
        if not last:
            xc, uc = matmul_res_ln(yc, w_out, xc, mod_c(i, 2), ln_g[i, 0], ln_b[i, 0], mod_c(i, 4), mod_c(i, 3))
            xc, uc = mlp_res_ln(uc, w1, w2, xc, mod_c(i, 5), ln_g[i, 1], ln_b[i, 1],
                                mod_c(i + 1, 1), mod_c(i + 1, 0))
    return xl.astype(x.dtype)
```
